```python
import math
import jax, jax.numpy as jnp
from jax import lax
import numpy as np

D_MODEL = 4096
BATCH = 4
SEQ = 4096
DEPTH = 1

RMS_EPS = 1e-5
SSM_WIDTH = D_MODEL // 2
SSM_GROUP = 16
SSM_GROUPS = SSM_WIDTH // SSM_GROUP
SSM_STATE = 64
DT_MIN = 1e-3
DT_MAX = 1e-1
HEAD_DIM = 128
ATTN_HEADS = D_MODEL // 512
DILATION_PATTERNS = ((128, 1), (512, 4), (2048, 16))
N_ATTN_GROUPS = len(DILATION_PATTERNS)
ATTN_GROUP_WIDTH = ATTN_HEADS * HEAD_DIM
ATTN_QKV_WIDTH = N_ATTN_GROUPS * ATTN_GROUP_WIDTH
ROPE_THETA = 500000.0
ROPE_DIM = HEAD_DIM // 4
D_FF = 256 * ((8 * D_MODEL // 3 + 255) // 256)
CONV_WIDTH = 3
IN_SPLITS = (SSM_WIDTH, ATTN_QKV_WIDTH, ATTN_QKV_WIDTH, ATTN_QKV_WIDTH, D_MODEL, D_MODEL)
IN_COLS = sum(IN_SPLITS)

kernel_name = "hybrid_s5_dilated_attn_gated_block"


def rms_norm(x, g):
    xf = x.astype(jnp.float32)
    y = xf * lax.rsqrt(jnp.mean(xf * xf, axis=-1, keepdims=True) + RMS_EPS)
    return (y * g.astype(jnp.float32)).astype(x.dtype)


def partial_rotary(t):
    seq = t.shape[1]
    half = ROPE_DIM // 2
    inv_freq = ROPE_THETA ** (-jnp.arange(0, ROPE_DIM, 2, dtype=jnp.float32) / ROPE_DIM)
    ang = jnp.arange(seq, dtype=jnp.float32)[:, None] * inv_freq[None, :]
    cos = jnp.cos(ang)[None, :, None, None, :]
    sin = jnp.sin(ang)[None, :, None, None, :]
    tf = t.astype(jnp.float32)
    t1, t2, rest = tf[..., :half], tf[..., half:ROPE_DIM], tf[..., ROPE_DIM:]
    out = jnp.concatenate([t1 * cos - t2 * sin, t2 * cos + t1 * sin, rest], axis=-1)
    return out.astype(t.dtype)


def dilated_window_attention(q, k, v, dilation, n_back):
    b, l, h, dh = q.shape
    n = l // dilation
    blk = n_back
    nb = -(-n // blk)
    n_pad = nb * blk

    def by_residue(t):
        return jnp.transpose(t.reshape(b, n, dilation, h, dh), (0, 2, 3, 1, 4))

    qr, kr, vr = by_residue(q), by_residue(k), by_residue(v)
    qb = jnp.pad(qr, ((0, 0), (0, 0), (0, 0), (0, n_pad - n), (0, 0)))
    qb = qb.reshape(b, dilation, h, nb, blk, dh)

    def key_blocks(t):
        tp = jnp.pad(t, ((0, 0), (0, 0), (0, 0), (blk, n_pad - n), (0, 0)))
        prev = tp[..., :n_pad, :].reshape(b, dilation, h, nb, blk, dh)
        cur = tp[..., blk:, :].reshape(b, dilation, h, nb, blk, dh)
        return jnp.concatenate([prev, cur], axis=-2)

    kb, vb = key_blocks(kr), key_blocks(vr)
    s = jnp.einsum('bdhnqc,bdhnkc->bdhnqk', qb, kb,
                   preferred_element_type=jnp.float32) * (dh ** -0.5)
    qi = jnp.arange(blk)[:, None]
    kj = jnp.arange(2 * blk)[None, :]
    dist = blk + qi - kj
    band = (dist >= 0) & (dist <= n_back)
    has_key = (jnp.arange(nb)[:, None, None] > 0) | (kj >= blk)[None]
    mask = band[None] & has_key
    s = jnp.where(mask, s, -jnp.inf)
    lse = jax.nn.logsumexp(s, axis=-1)
    p = jnp.exp(s - lse[..., None])
    o = jnp.einsum('bdhnqk,bdhnkc->bdhnqc', p, vb.astype(jnp.float32))
    o = o.reshape(b, dilation, h, n_pad, dh)[..., :n, :]
    lse = lse.reshape(b, dilation, h, n_pad)[..., :n]
    o = jnp.transpose(o, (0, 3, 1, 2, 4)).reshape(b, l, h, dh)
    lse = jnp.transpose(lse, (0, 3, 1, 2)).reshape(b, l, h)
    return o, lse


def complex_linear_combine(e1, e2):
    a1r, a1i, b1r, b1i = e1
    a2r, a2i, b2r, b2i = e2
    ar = a2r * a1r - a2i * a1i
    ai = a2r * a1i + a2i * a1r
    br = a2r * b1r - a2i * b1i + b2r
    bi = a2r * b1i + a2i * b1r + b2i
    return (ar, ai, br, bi)


def s5_mixer(u, log_dt, a_re, a_im, b_re, b_im, c_re, c_im, d_skip):
    f32 = jnp.float32
    uf = u.astype(f32)
    a_re, a_im = a_re.astype(f32), a_im.astype(f32)
    dt = jnp.exp(log_dt.astype(f32))[:, None]
    mag = jnp.exp(dt * a_re)
    lb_re, lb_im = mag * jnp.cos(dt * a_im), mag * jnp.sin(dt * a_im)
    den = a_re * a_re + a_im * a_im
    f_re = ((lb_re - 1.0) * a_re + lb_im * a_im) / den
    f_im = (lb_im * a_re - (lb_re - 1.0) * a_im) / den
    br, bi = b_re.astype(f32), b_im.astype(f32)
    bb_re = f_re[..., None] * br - f_im[..., None] * bi
    bb_im = f_re[..., None] * bi + f_im[..., None] * br
    cr, ci = c_re.astype(f32), c_im.astype(f32)
    dsk = d_skip.astype(f32)

    def one_sequence(us):
        bu_re = jnp.einsum('gpc,lgc->lgp', bb_re, us)
        bu_im = jnp.einsum('gpc,lgc->lgp', bb_im, us)
        ar = jnp.broadcast_to(lb_re, bu_re.shape)
        ai = jnp.broadcast_to(lb_im, bu_re.shape)
        _, _, x_re, x_im = lax.associative_scan(
            complex_linear_combine, (ar, ai, bu_re, bu_im), axis=0)
        y = (jnp.einsum('gcp,lgp->lgc', cr, x_re)
             - jnp.einsum('gcp,lgp->lgc', ci, x_im))
        return y + dsk * us

    return lax.map(one_sequence, uf)


def setup_inputs(seed: int = 0) -> dict:
    key = jax.random.key(seed)
    ks = jax.random.split(key, 20)
    f32 = jnp.float32

    def nrm(k, shape, fan_in):
        return jax.random.normal(k, shape, f32) * (fan_in ** -0.5)

    G, P, C = SSM_GROUPS, SSM_STATE, SSM_GROUP
    x = jax.random.normal(ks[0], (BATCH, SEQ, D_MODEL), f32)
    g_mix = 1.0 + 0.02 * jax.random.normal(ks[1], (DEPTH, D_MODEL), f32)
    w_in = nrm(ks[2], (DEPTH, D_MODEL, IN_COLS), D_MODEL)
    ssm_log_dt = jax.random.uniform(ks[3], (DEPTH, G), f32,
                                    minval=math.log(DT_MIN), maxval=math.log(DT_MAX))
    ssm_a_re = -0.5 + 0.01 * jax.random.normal(ks[4], (DEPTH, G, P), f32)
    ssm_a_im = (math.pi * jnp.arange(P, dtype=f32))[None, None, :] \
        + 0.01 * jax.random.normal(ks[5], (DEPTH, G, P), f32)
    ssm_b_re = nrm(ks[6], (DEPTH, G, P, C), 2 * C)
    ssm_b_im = nrm(ks[7], (DEPTH, G, P, C), 2 * C)
    ssm_c_re = nrm(ks[8], (DEPTH, G, C, P), 2 * P)
    ssm_c_im = nrm(ks[9], (DEPTH, G, C, P), 2 * P)
    ssm_d = jax.random.normal(ks[10], (DEPTH, G, C), f32)
    w_glu = nrm(ks[11], (DEPTH, SSM_WIDTH, 2 * D_MODEL), SSM_WIDTH)
    w_attn_out = nrm(ks[12], (DEPTH, ATTN_GROUP_WIDTH, D_MODEL), ATTN_GROUP_WIDTH)
    w_out = nrm(ks[13], (DEPTH, D_MODEL, D_MODEL), D_MODEL)
    g_ffn = 1.0 + 0.02 * jax.random.normal(ks[14], (DEPTH, D_MODEL), f32)
    w_up = nrm(ks[15], (DEPTH, D_MODEL, 2 * D_FF), D_MODEL)
    conv_w = nrm(ks[16], (DEPTH, CONV_WIDTH, 2 * D_FF), CONV_WIDTH)
    conv_b = 0.01 * jax.random.normal(ks[17], (DEPTH, 2 * D_FF), f32)
    w_down = nrm(ks[18], (DEPTH, D_FF, D_MODEL), D_FF)
    g_final = 1.0 + 0.02 * jax.random.normal(ks[19], (D_MODEL,), f32)
    return {"x": x, "g_mix": g_mix, "w_in": w_in, "ssm_log_dt": ssm_log_dt,
            "ssm_a_re": ssm_a_re, "ssm_a_im": ssm_a_im, "ssm_b_re": ssm_b_re,
            "ssm_b_im": ssm_b_im, "ssm_c_re": ssm_c_re, "ssm_c_im": ssm_c_im,
            "ssm_d": ssm_d, "w_glu": w_glu, "w_attn_out": w_attn_out, "w_out": w_out,
            "g_ffn": g_ffn, "w_up": w_up, "conv_w": conv_w, "conv_b": conv_b,
            "w_down": w_down, "g_final": g_final}


def reference(x, g_mix, w_in, ssm_log_dt, ssm_a_re, ssm_a_im, ssm_b_re, ssm_b_im,
              ssm_c_re, ssm_c_im, ssm_d, w_glu, w_attn_out, w_out, g_ffn, w_up,
              conv_w, conv_b, w_down, g_final):
    b, l, _ = x.shape
    split_points = list(np.cumsum(IN_SPLITS)[:-1])
    for i in range(DEPTH):
        h = rms_norm(x, g_mix[i])
        proj = h @ w_in[i]
        u, q, k, v, gate_s, gate_a = jnp.split(proj, split_points, axis=-1)

        u = u.reshape(b, l, SSM_GROUPS, SSM_GROUP)
        y = s5_mixer(u, ssm_log_dt[i], ssm_a_re[i], ssm_a_im[i], ssm_b_re[i], ssm_b_im[i],
                     ssm_c_re[i], ssm_c_im[i], ssm_d[i])
        y = jax.nn.gelu(y.reshape(b, l, SSM_WIDTH)).astype(x.dtype)
        glu_a, glu_b = jnp.split(y @ w_glu[i], 2, axis=-1)
        ssm_branch = glu_a * jax.nn.sigmoid(glu_b)

        shp = (b, l, N_ATTN_GROUPS, ATTN_HEADS, HEAD_DIM)
        q = partial_rotary(q.reshape(shp))
        k = partial_rotary(k.reshape(shp))
        v = v.reshape(shp)
        outs, lses = [], []
        for g, (window, dilation) in enumerate(DILATION_PATTERNS):
            o_g, lse_g = dilated_window_attention(q[:, :, g], k[:, :, g], v[:, :, g],
                                                  dilation, window // dilation)
            outs.append(o_g)
            lses.append(lse_g)
        wts = jax.nn.softmax(jnp.stack(lses, axis=0), axis=0)
        attn = jnp.sum(wts[..., None] * jnp.stack(outs, axis=0), axis=0)
        attn = attn.reshape(b, l, ATTN_GROUP_WIDTH).astype(x.dtype)
        attn_branch = attn @ w_attn_out[i]

        mixed = jax.nn.sigmoid(gate_s) * ssm_branch + jax.nn.sigmoid(gate_a) * attn_branch
        x = x + mixed @ w_out[i]

        h = rms_norm(x, g_ffn[i])
        up = h @ w_up[i]
        up = lax.conv_general_dilated(
            up, conv_w[i][:, None, :].astype(up.dtype), window_strides=(1,),
            padding=[(CONV_WIDTH - 1, 0)], dimension_numbers=('NWC', 'WIO', 'NWC'),
            feature_group_count=2 * D_FF) + conv_b[i]
        a, gv = jnp.split(up, 2, axis=-1)
        x = x + (jax.nn.silu(a) * gv) @ w_down[i]
    return rms_norm(x, g_final)
```

```python
import functools
import math

import jax
import jax.numpy as jnp
from jax import lax
from jax.experimental import pallas as pl
from jax.experimental.pallas import tpu as pltpu

F32 = jnp.float32
BF16 = jnp.bfloat16

RMS_EPS = 1e-5
LANES = 128
HEAD_DIM = 128
ROPE_DIM = HEAD_DIM // 4
ROPE_THETA = 500000.0
DILATION_PATTERNS = ((128, 1), (512, 4), (2048, 16))
ATTN_BLOCK = 128
S5_CHUNK = 16
S5_SLAB_GROUPS = 8
CONV_WIDTH = 3
HALO_ROWS = 8
NEG_BIG = -1e30
VMEM_LIMIT = 56 * 1024 * 1024


def _params(n_axes):
    return pltpu.CompilerParams(dimension_semantics=("arbitrary",) * n_axes,
                                vmem_limit_bytes=VMEM_LIMIT)


def _rmsnorm_kernel(x_ref, g_ref, o_ref):
    x = x_ref[...]
    y = x * lax.rsqrt(jnp.mean(x * x, axis=-1, keepdims=True) + RMS_EPS)
    o_ref[...] = (y * g_ref[...]).astype(o_ref.dtype)


def rmsnorm(x2d, g, out_dtype, tm=256):
    m, d = x2d.shape
    return pl.pallas_call(
        _rmsnorm_kernel,
        grid=(m // tm,),
        in_specs=[pl.BlockSpec((tm, d), lambda i: (i, 0)),
                  pl.BlockSpec((1, d), lambda i: (0, 0))],
        out_specs=pl.BlockSpec((tm, d), lambda i: (i, 0)),
        out_shape=jax.ShapeDtypeStruct((m, d), out_dtype),
        compiler_params=_params(1),
        name="rmsnorm",
    )(x2d, g.reshape(1, d).astype(F32))


def _mm_kernel(a_ref, w_ref, o_ref):
    o_ref[...] = jnp.dot(a_ref[...], w_ref[...], preferred_element_type=F32).astype(o_ref.dtype)


def matmul(a, w, out_dtype, tm, tn, name):
    m, k = a.shape
    n = w.shape[1]
    return pl.pallas_call(
        _mm_kernel,
        grid=(m // tm, n // tn),
        in_specs=[pl.BlockSpec((tm, k), lambda i, j: (i, 0)),
                  pl.BlockSpec((k, tn), lambda i, j: (0, j))],
        out_specs=pl.BlockSpec((tm, tn), lambda i, j: (i, j)),
        out_shape=jax.ShapeDtypeStruct((m, n), out_dtype),
        compiler_params=_params(2),
        name=name,
    )(a, w)


def _mm_res_kernel(a_ref, w_ref, r_ref, o_ref):
    k = pl.program_id(2)
    part = jnp.dot(a_ref[...], w_ref[...], preferred_element_type=F32)

    @pl.when(k == 0)
    def _():
        o_ref[...] = r_ref[...] + part

    @pl.when(k != 0)
    def _():
        o_ref[...] += part


def matmul_residual(a, w, res, tm, tn, nk, name):
    m, k = a.shape
    n = w.shape[1]
    tk = k // nk
    return pl.pallas_call(
        _mm_res_kernel,
        grid=(m // tm, n // tn, nk),
        in_specs=[pl.BlockSpec((tm, tk), lambda i, j, kk: (i, kk)),
                  pl.BlockSpec((tk, tn), lambda i, j, kk: (kk, j)),
                  pl.BlockSpec((tm, tn), lambda i, j, kk: (i, j))],
        out_specs=pl.BlockSpec((tm, tn), lambda i, j, kk: (i, j)),
        out_shape=jax.ShapeDtypeStruct((m, n), F32),
        compiler_params=_params(3),
        name=name,
    )(a, w, res)


def _gelu_tanh(x):
    c = math.sqrt(2.0 / math.pi)
    return 0.5 * x * (1.0 + jnp.tanh(c * (x + 0.044715 * (x * x * x))))


def _s5_kernel(u_ref, m_ref, w_ref, v_ref, a_ref, o_ref, uf_scr, yf_scr, bc_scr, sin_scr, *, rows):
    half = bc_scr.shape[1] // 2
    uf_scr[...] = u_ref[...].astype(F32)
    planes = [uf_scr[pl.ds(t, rows, stride=S5_CHUNK), :].astype(BF16) for t in range(S5_CHUNK)]
    u = jnp.concatenate(planes, axis=1)
    y_intra = jnp.dot(u, m_ref[...], preferred_element_type=F32)
    bc_scr[...] = jnp.dot(u, w_ref[...], preferred_element_type=F32)

    ar = a_ref[0:1, :]
    ai = a_ref[1:2, :]

    def tile_body(i, carry):
        sr, si = carry
        base = pl.multiple_of(i * 8, 8)
        tile = bc_scr[pl.ds(base, 8), :]
        rows_r, rows_i = [], []
        for r in range(8):
            rows_r.append(sr)
            rows_i.append(si)
            br = tile[r:r + 1, :half]
            bi = tile[r:r + 1, half:]
            sr, si = ar * sr - ai * si + br, ar * si + ai * sr + bi
        sin_scr[pl.ds(base, 8), :] = jnp.concatenate(
            [jnp.concatenate(rows_r, axis=0), jnp.concatenate(rows_i, axis=0)], axis=1)
        return sr, si

    zero = jnp.zeros((1, half), F32)
    lax.fori_loop(0, rows // 8, tile_body, (zero, zero))

    y_state = jnp.dot(sin_scr[...].astype(BF16), v_ref[...], preferred_element_type=F32)
    y = _gelu_tanh(y_intra + y_state)
    for t in range(S5_CHUNK):
        yf_scr[pl.ds(t, rows, stride=S5_CHUNK), :] = y[:, t * LANES:(t + 1) * LANES]
    o_ref[...] = yf_scr[...].astype(o_ref.dtype)


def s5_prepare(log_dt, a_re, a_im, b_re, b_im, c_re, c_im, d_skip):
    hp = lax.Precision.HIGHEST
    g, p = a_re.shape
    c = b_re.shape[-1]
    t_ = S5_CHUNK
    gl = S5_SLAB_GROUPS
    ns = g // gl
    a_re, a_im = a_re.astype(F32), a_im.astype(F32)
    dt = jnp.exp(log_dt.astype(F32))[:, None]
    ks = jnp.arange(t_ + 1, dtype=F32)[:, None, None]
    mag = jnp.exp(ks * (dt * a_re)[None])
    ang = ks * (dt * a_im)[None]
    pr, pi = mag * jnp.cos(ang), mag * jnp.sin(ang)
    lb_re, lb_im = pr[1], pi[1]
    den = a_re * a_re + a_im * a_im
    f_re = ((lb_re - 1.0) * a_re + lb_im * a_im) / den
    f_im = (lb_im * a_re - (lb_re - 1.0) * a_im) / den
    br, bi = b_re.astype(F32), b_im.astype(F32)
    bb_re = f_re[..., None] * br - f_im[..., None] * bi
    bb_im = f_re[..., None] * bi + f_im[..., None] * br
    cr, ci = c_re.astype(F32), c_im.astype(F32)
    cp_re = cr[None] * pr[:, :, None, :] - ci[None] * pi[:, :, None, :]
    cp_im = cr[None] * pi[:, :, None, :] + ci[None] * pr[:, :, None, :]
    kk = (jnp.einsum('kgcp,gpd->kgcd', cp_re[:t_], bb_re, precision=hp)
          - jnp.einsum('kgcp,gpd->kgcd', cp_im[:t_], bb_im, precision=hp))
    tin = jnp.arange(t_)[:, None]
    tout = jnp.arange(t_)[None, :]
    lag = tout - tin
    kt = kk[jnp.clip(lag, 0, t_ - 1)]
    kt = jnp.where((lag >= 0)[:, :, None, None, None], kt, 0.0)
    eye_t = jnp.eye(t_, dtype=F32)
    eye_c = jnp.eye(c, dtype=F32)
    kt = kt + eye_t[:, :, None, None, None] * eye_c[None, None, None] * d_skip.astype(F32)[None, None, :, :, None]
    eye_g = jnp.eye(gl, dtype=F32)
    kt = kt.reshape(t_, t_, ns, gl, c, c)
    m_mat = jnp.einsum('abshoi,hk->sahibko', kt, eye_g).reshape(ns, t_ * gl * c, t_ * gl * c)
    prr = pr[t_ - 1::-1][:t_]
    pir = pi[t_ - 1::-1][:t_]
    w_re = prr[:, :, :, None] * bb_re[None] - pir[:, :, :, None] * bb_im[None]
    w_im = prr[:, :, :, None] * bb_im[None] + pir[:, :, :, None] * bb_re[None]
    wc = jnp.stack([w_re, w_im], axis=0).reshape(2, t_, ns, gl, p, c)
    w_mat = jnp.einsum('rasgpi,gk->sagirkp', wc, eye_g).reshape(ns, t_ * gl * c, 2 * gl * p)
    vc = jnp.stack([cp_re[1:], -cp_im[1:]], axis=0).reshape(2, t_, ns, gl, c, p)
    v_mat = jnp.einsum('rbsgop,gk->srgpbko', vc, eye_g).reshape(ns, 2 * gl * p, t_ * gl * c)
    a_mat = jnp.stack([pr[t_], pi[t_]], axis=0).reshape(2, ns, gl * p).transpose(1, 0, 2)
    return m_mat.astype(BF16), w_mat.astype(BF16), v_mat.astype(BF16), a_mat.astype(F32)


def s5_mixer_gelu(proj, col_block0, n_batch, seq, mats):
    m_mat, w_mat, v_mat, a_mat = mats
    ns = m_mat.shape[0]
    kdim = m_mat.shape[1]
    sdim = w_mat.shape[2]
    rows = seq // S5_CHUNK
    return pl.pallas_call(
        functools.partial(_s5_kernel, rows=rows),
        grid=(ns, n_batch),
        in_specs=[pl.BlockSpec((seq, LANES), lambda s, b: (b, col_block0 + s)),
                  pl.BlockSpec((None, kdim, kdim), lambda s, b: (s, 0, 0)),
                  pl.BlockSpec((None, kdim, sdim), lambda s, b: (s, 0, 0)),
                  pl.BlockSpec((None, sdim, kdim), lambda s, b: (s, 0, 0)),
                  pl.BlockSpec((None, 2, sdim // 2), lambda s, b: (s, 0, 0))],
        out_specs=pl.BlockSpec((seq, LANES), lambda s, b: (b, s)),
        out_shape=jax.ShapeDtypeStruct((n_batch * seq, ns * LANES), BF16),
        scratch_shapes=[pltpu.VMEM((seq, LANES), F32),
                        pltpu.VMEM((seq, LANES), F32),
                        pltpu.VMEM((rows, sdim), F32),
                        pltpu.VMEM((rows, sdim), F32)],
        compiler_params=_params(2),
        name="s5_mixer",
    )(proj, m_mat, w_mat, v_mat, a_mat)


def rope_tables(seq):
    half = ROPE_DIM // 2
    inv_freq = ROPE_THETA ** (-jnp.arange(0, ROPE_DIM, 2, dtype=F32) / ROPE_DIM)
    ang = jnp.arange(seq, dtype=F32)[:, None] * inv_freq[None, :]
    cos, sin = jnp.cos(ang), jnp.sin(ang)
    pad = HEAD_DIM - ROPE_DIM
    cos_t = jnp.concatenate([cos, cos, jnp.ones((seq, pad), F32)], axis=1)
    zeros_h = jnp.zeros((seq, half), F32)
    zeros_p = jnp.zeros((seq, pad), F32)
    sin_lo = jnp.concatenate([-sin, zeros_h, zeros_p], axis=1)
    sin_hi = jnp.concatenate([zeros_h, sin, zeros_p], axis=1)
    return cos_t, sin_lo, sin_hi


def _rotary(x, cos_t, sin_lo, sin_hi):
    half = ROPE_DIM // 2
    up = pltpu.roll(x, HEAD_DIM - half, axis=1)
    dn = pltpu.roll(x, half, axis=1)
    return x * cos_t + up * sin_lo + dn * sin_hi


def _attn_kernel(*refs, seq):
    n_g = len(DILATION_PATTERNS)
    qkv_refs = refs[:3 * n_g]
    cos_ref, slo_ref, shi_ref = refs[3 * n_g:3 * n_g + 3]
    o_ref = refs[3 * n_g + 3]
    qf, kf, vf, acc_scr, m_scr, l_scr = refs[3 * n_g + 4:]
    blk = ATTN_BLOCK
    scale = HEAD_DIM ** -0.5
    row = lax.broadcasted_iota(jnp.int32, (blk, blk), 0)
    col = lax.broadcasted_iota(jnp.int32, (blk, blk), 1)
    cur_ok = col <= row
    prev_ok = col >= row
    nt = (((1,), (1,)), ((), ()))

    for g, (window, dil) in enumerate(DILATION_PATTERNS):
        assert window // dil == blk
        q_ref, k_ref, v_ref = qkv_refs[3 * g:3 * g + 3]
        cos_t, sin_lo, sin_hi = cos_ref[...], slo_ref[...], shi_ref[...]
        qf[...] = _rotary(q_ref[...].astype(F32), cos_t, sin_lo, sin_hi) * scale
        kf[...] = _rotary(k_ref[...].astype(F32), cos_t, sin_lo, sin_hi)
        vf[...] = v_ref[...].astype(F32)
        n_qb = seq // dil // blk

        def strided(ref, start):
            if dil == 1:
                return ref[pl.ds(start, blk), :]
            return ref[pl.ds(start, blk, stride=dil), :]

        def strided_store(ref, start, val):
            if dil == 1:
                ref[pl.ds(start, blk), :] = val
            else:
                ref[pl.ds(start, blk, stride=dil), :] = val

        def block(res, qb, first=(g == 0), strided=strided, strided_store=strided_store, dil=dil):
            base = qb * (blk * dil) + res
            pbase = jnp.maximum(base - blk * dil, res)
            q = strided(qf, base).astype(BF16)
            kc = strided(kf, base).astype(BF16)
            kp = strided(kf, pbase).astype(BF16)
            vc = strided(vf, base).astype(BF16)
            vp = strided(vf, pbase).astype(BF16)
            s_c = lax.dot_general(q, kc, nt, preferred_element_type=F32)
            s_p = lax.dot_general(q, kp, nt, preferred_element_type=F32)
            s_c = jnp.where(cur_ok, s_c, NEG_BIG)
            s_p = jnp.where(jnp.logical_and(prev_ok, qb > 0), s_p, NEG_BIG)
            m_blk = jnp.maximum(jnp.max(s_c, axis=1, keepdims=True), jnp.max(s_p, axis=1, keepdims=True))
            if first:
                m_new = jnp.broadcast_to(m_blk, (blk, blk))
            else:
                m_old = strided(m_scr, base)
                m_new = jnp.maximum(m_old, m_blk)
            p_c = jnp.exp(s_c - m_new)
            p_p = jnp.exp(s_p - m_new)
            l_blk = jnp.sum(p_c, axis=1, keepdims=True) + jnp.sum(p_p, axis=1, keepdims=True)
            pv = (jnp.dot(p_c.astype(BF16), vc, preferred_element_type=F32)
                  + jnp.dot(p_p.astype(BF16), vp, preferred_element_type=F32))
            if first:
                acc_new = pv
                l_new = jnp.broadcast_to(l_blk, (blk, blk))
            else:
                alpha = jnp.exp(m_old - m_new)
                acc_new = alpha * strided(acc_scr, base) + pv
                l_new = alpha * strided(l_scr, base) + l_blk
            strided_store(acc_scr, base, acc_new)
            strided_store(m_scr, base, m_new)
            strided_store(l_scr, base, l_new)

        def res_body(res, _, block=block, n_qb=n_qb):
            def pair_body(i, _):
                block(res, 2 * i)
                block(res, 2 * i + 1)
                return 0
            lax.fori_loop(0, n_qb // 2, pair_body, 0)
            return 0

        lax.fori_loop(0, dil, res_body, 0)

    o_ref[...] = (acc_scr[...] / l_scr[...]).astype(o_ref.dtype)


def dilated_attention(proj, cb_q, cb_k, cb_v, n_batch, seq, n_heads):
    n_g = len(DILATION_PATTERNS)
    tables = rope_tables(seq)

    def col_spec(cb0, g):
        return pl.BlockSpec((seq, LANES), lambda b, h: (b, cb0 + g * n_heads + h))

    in_specs = []
    for g in range(n_g):
        in_specs += [col_spec(cb_q, g), col_spec(cb_k, g), col_spec(cb_v, g)]
    in_specs += [pl.BlockSpec((seq, LANES), lambda b, h: (0, 0))] * 3
    return pl.pallas_call(
        functools.partial(_attn_kernel, seq=seq),
        grid=(n_batch, n_heads),
        in_specs=in_specs,
        out_specs=pl.BlockSpec((seq, LANES), lambda b, h: (b, h)),
        out_shape=jax.ShapeDtypeStruct((n_batch * seq, n_heads * LANES), BF16),
        scratch_shapes=[pltpu.VMEM((seq, LANES), F32)] * 6,
        compiler_params=_params(2),
        name="dilated_attention",
    )(*([proj] * (3 * n_g)), *tables)


def _sigmoid(x):
    return 1.0 / (1.0 + jnp.exp(-x))


def _mix_kernel(y_ref, at_ref, wa_ref, wb_ref, wo_ref, gs_ref, ga_ref, o_ref):
    y = y_ref[...]
    glu_a = jnp.dot(y, wa_ref[...], preferred_element_type=F32)
    glu_b = jnp.dot(y, wb_ref[...], preferred_element_type=F32)
    attn = jnp.dot(at_ref[...], wo_ref[...], preferred_element_type=F32)
    ssm = glu_a * _sigmoid(glu_b)
    mixed = _sigmoid(gs_ref[...].astype(F32)) * ssm + _sigmoid(ga_ref[...].astype(F32)) * attn
    o_ref[...] = mixed.astype(o_ref.dtype)


def gated_merge(y, attn, w_glu, w_ao, proj, col_gs, col_ga, tm, tn):
    m, ky = y.shape
    ka = attn.shape[1]
    d = w_ao.shape[1]
    nj = d // tn
    return pl.pallas_call(
        _mix_kernel,
        grid=(m // tm, nj),
        in_specs=[pl.BlockSpec((tm, ky), lambda i, j: (i, 0)),
                  pl.BlockSpec((tm, ka), lambda i, j: (i, 0)),
                  pl.BlockSpec((ky, tn), lambda i, j: (0, j)),
                  pl.BlockSpec((ky, tn), lambda i, j: (0, nj + j)),
                  pl.BlockSpec((ka, tn), lambda i, j: (0, j)),
                  pl.BlockSpec((tm, tn), lambda i, j: (i, col_gs // tn + j)),
                  pl.BlockSpec((tm, tn), lambda i, j: (i, col_ga // tn + j))],
        out_specs=pl.BlockSpec((tm, tn), lambda i, j: (i, j)),
        out_shape=jax.ShapeDtypeStruct((m, d), BF16),
        compiler_params=_params(2),
        name="gated_merge",
    )(y, attn, w_glu, w_glu, w_ao, proj, proj)


def _ffn_up_kernel(h_ref, wa_ref, wg_ref, ca_ref, cg_ref, o_ref, ua_scr, ug_scr, halo_a, halo_g,
                   *, tm, blocks_per_seq):
    i = pl.program_id(0)
    j = pl.program_id(1)
    h = h_ref[...]
    seq_start = (i % blocks_per_seq) == 0

    def conv(w_ref, c_ref, u_scr, halo):
        u_scr[pl.ds(HALO_ROWS, tm), :] = jnp.dot(h, w_ref[...], preferred_element_type=F32)
        u_scr[pl.ds(0, HALO_ROWS), :] = jnp.where(seq_start, 0.0, halo[j])
        halo[j] = u_scr[pl.ds(tm, HALO_ROWS), :]
        cw = c_ref[...]
        return (cw[2:3, :] * u_scr[pl.ds(HALO_ROWS, tm), :]
                + cw[1:2, :] * u_scr[pl.ds(HALO_ROWS - 1, tm), :]
                + cw[0:1, :] * u_scr[pl.ds(HALO_ROWS - 2, tm), :]
                + cw[3:4, :])

    a = conv(wa_ref, ca_ref, ua_scr, halo_a)
    gv = conv(wg_ref, cg_ref, ug_scr, halo_g)
    o_ref[...] = (a * _sigmoid(a) * gv).astype(o_ref.dtype)


def ffn_up(h, wa, wg, ca, cg, seq, tm, tn):
    m, k = h.shape
    f = wa.shape[1]
    nj = f // tn
    return pl.pallas_call(
        functools.partial(_ffn_up_kernel, tm=tm, blocks_per_seq=seq // tm),
        grid=(m // tm, nj),
        in_specs=[pl.BlockSpec((tm, k), lambda i, j: (i, 0)),
                  pl.BlockSpec((k, tn), lambda i, j: (0, j)),
                  pl.BlockSpec((k, tn), lambda i, j: (0, j)),
                  pl.BlockSpec((HALO_ROWS, tn), lambda i, j: (0, j)),
                  pl.BlockSpec((HALO_ROWS, tn), lambda i, j: (0, j))],
        out_specs=pl.BlockSpec((tm, tn), lambda i, j: (i, j)),
        out_shape=jax.ShapeDtypeStruct((m, f), BF16),
        scratch_shapes=[pltpu.VMEM((tm + HALO_ROWS, tn), F32),
                        pltpu.VMEM((tm + HALO_ROWS, tn), F32),
                        pltpu.VMEM((nj, HALO_ROWS, tn), F32),
                        pltpu.VMEM((nj, HALO_ROWS, tn), F32)],
        compiler_params=_params(2),
        name="ffn_up",
    )(h, wa, wg, ca, cg)


def _pad_cols(w, n):
    return jnp.pad(w, ((0, 0), (0, n - w.shape[1])))


def kernel(x, g_mix, w_in, ssm_log_dt, ssm_a_re, ssm_a_im, ssm_b_re, ssm_b_im, ssm_c_re, ssm_c_im,
           ssm_d, w_glu, w_attn_out, w_out, g_ffn, w_up, conv_w, conv_b, w_down, g_final):
    b, l, d = x.shape
    depth = g_mix.shape[0]
    ssm_width = ssm_a_re.shape[1] * ssm_b_re.shape[-1]
    attn_width = w_attn_out.shape[1]
    n_heads = attn_width // HEAD_DIM
    qkv_width = len(DILATION_PATTERNS) * attn_width
    d_ff = w_down.shape[1]
    ff_tile = 512
    d_ff_p = -(-d_ff // ff_tile) * ff_tile
    col_q = ssm_width
    col_k = col_q + qkv_width
    col_v = col_k + qkv_width
    col_gs = col_v + qkv_width
    col_ga = col_gs + d

    xf = x.reshape(b * l, d).astype(F32)
    for i in range(depth):
        h = rmsnorm(xf, g_mix[i], BF16)
        proj = matmul(h, w_in[i].astype(BF16), BF16, 1024, 1024, "in_proj")
        mats = s5_prepare(ssm_log_dt[i], ssm_a_re[i], ssm_a_im[i], ssm_b_re[i], ssm_b_im[i],
                          ssm_c_re[i], ssm_c_im[i], ssm_d[i])
        y = s5_mixer_gelu(proj, 0, b, l, mats)
        attn = dilated_attention(proj, col_q // LANES, col_k // LANES, col_v // LANES, b, l, n_heads)
        mixed = gated_merge(y, attn, w_glu[i].astype(BF16), w_attn_out[i].astype(BF16), proj,
                            col_gs, col_ga, 1024, 512)
        xf = matmul_residual(mixed, w_out[i].astype(BF16), xf, 1024, 512, 1, "out_proj")

        h = rmsnorm(xf, g_ffn[i], BF16)
        wa = _pad_cols(w_up[i][:, :d_ff], d_ff_p).astype(BF16)
        wg = _pad_cols(w_up[i][:, d_ff:], d_ff_p).astype(BF16)
        zrow = jnp.zeros((HALO_ROWS - CONV_WIDTH - 1, d_ff_p), F32)
        ca = jnp.concatenate([_pad_cols(conv_w[i][:, :d_ff], d_ff_p),
                              _pad_cols(conv_b[i][None, :d_ff], d_ff_p), zrow], axis=0).astype(F32)
        cg = jnp.concatenate([_pad_cols(conv_w[i][:, d_ff:], d_ff_p),
                              _pad_cols(conv_b[i][None, d_ff:], d_ff_p), zrow], axis=0).astype(F32)
        act = ffn_up(h, wa, wg, ca, cg, l, 1024, ff_tile)
        wd = jnp.pad(w_down[i], ((0, d_ff_p - d_ff), (0, 0))).astype(BF16)
        xf = matmul_residual(act, wd, xf, 1024, 512, 2, "ffn_down")
    out = rmsnorm(xf, g_final, x.dtype)
    return out.reshape(b, l, d)
```

```python
import functools
import math

import jax
import jax.numpy as jnp
from jax import lax
from jax.experimental import pallas as pl
from jax.experimental.pallas import tpu as pltpu

F32 = jnp.float32
BF16 = jnp.bfloat16

RMS_EPS = 1e-5
LANES = 128
HEAD_DIM = 128
ROPE_DIM = HEAD_DIM // 4
ROPE_THETA = 500000.0
DILATION_PATTERNS = ((128, 1), (512, 4), (2048, 16))
ATTN_BLOCK = 128
ATTN_UNROLL = 8
S5_CHUNK = 16
S5_SLAB_GROUPS = 8
CONV_WIDTH = 3
HALO_ROWS = 8
NEG_BIG = -1e30
VMEM_LIMIT = 56 * 1024 * 1024


def _params(n_axes):
    return pltpu.CompilerParams(dimension_semantics=("arbitrary",) * n_axes,
                                vmem_limit_bytes=VMEM_LIMIT)


def _rmsnorm_kernel(x_ref, g_ref, o_ref):
    x = x_ref[...]
    y = x * lax.rsqrt(jnp.mean(x * x, axis=-1, keepdims=True) + RMS_EPS)
    o_ref[...] = (y * g_ref[...]).astype(o_ref.dtype)


def rmsnorm(x2d, g, out_dtype, tm=256):
    m, d = x2d.shape
    return pl.pallas_call(
        _rmsnorm_kernel,
        grid=(m // tm,),
        in_specs=[pl.BlockSpec((tm, d), lambda i: (i, 0)),
                  pl.BlockSpec((1, d), lambda i: (0, 0))],
        out_specs=pl.BlockSpec((tm, d), lambda i: (i, 0)),
        out_shape=jax.ShapeDtypeStruct((m, d), out_dtype),
        compiler_params=_params(1),
        name="rmsnorm",
    )(x2d, g.reshape(1, d).astype(F32))


def _mm_kernel(a_ref, w_ref, o_ref):
    o_ref[...] = jnp.dot(a_ref[...], w_ref[...], preferred_element_type=F32).astype(o_ref.dtype)


def matmul(a, w, out_dtype, tm, tn, n_col_tiles, w_col_tile, name):
    m, k = a.shape
    return pl.pallas_call(
        _mm_kernel,
        grid=(m // tm, n_col_tiles),
        in_specs=[pl.BlockSpec((tm, k), lambda i, j: (i, 0)),
                  pl.BlockSpec((k, tn), lambda i, j: (0, w_col_tile(j)))],
        out_specs=pl.BlockSpec((tm, tn), lambda i, j: (i, j)),
        out_shape=jax.ShapeDtypeStruct((m, n_col_tiles * tn), out_dtype),
        compiler_params=_params(2),
        name=name,
    )(a, w)


def _mm_res_kernel(a_ref, w_ref, r_ref, o_ref):
    k = pl.program_id(2)
    part = jnp.dot(a_ref[...], w_ref[...], preferred_element_type=F32)

    @pl.when(k == 0)
    def _():
        o_ref[...] = r_ref[...] + part

    @pl.when(k != 0)
    def _():
        o_ref[...] += part


def matmul_residual(a, w, res, tm, tn, nk, name):
    m, k = a.shape
    n = w.shape[1]
    tk = k // nk
    return pl.pallas_call(
        _mm_res_kernel,
        grid=(m // tm, n // tn, nk),
        in_specs=[pl.BlockSpec((tm, tk), lambda i, j, kk: (i, kk)),
                  pl.BlockSpec((tk, tn), lambda i, j, kk: (kk, j)),
                  pl.BlockSpec((tm, tn), lambda i, j, kk: (i, j))],
        out_specs=pl.BlockSpec((tm, tn), lambda i, j, kk: (i, j)),
        out_shape=jax.ShapeDtypeStruct((m, n), F32),
        compiler_params=_params(3),
        name=name,
    )(a, w, res)


def rope_tables(seq):
    half = ROPE_DIM // 2
    inv_freq = ROPE_THETA ** (-jnp.arange(0, ROPE_DIM, 2, dtype=F32) / ROPE_DIM)
    ang = jnp.arange(seq, dtype=F32)[:, None] * inv_freq[None, :]
    cos, sin = jnp.cos(ang), jnp.sin(ang)
    pad = HEAD_DIM - ROPE_DIM
    cos_t = jnp.concatenate([cos, cos, jnp.ones((seq, pad), F32)], axis=1)
    zeros_h = jnp.zeros((seq, half), F32)
    zeros_p = jnp.zeros((seq, pad), F32)
    sin_lo = jnp.concatenate([-sin, zeros_h, zeros_p], axis=1)
    sin_hi = jnp.concatenate([zeros_h, sin, zeros_p], axis=1)
    return cos_t, sin_lo, sin_hi


def _rotary(x, cos_t, sin_lo, sin_hi):
    half = ROPE_DIM // 2
    up = pltpu.roll(x, HEAD_DIM - half, axis=1)
    dn = pltpu.roll(x, half, axis=1)
    return x * cos_t + up * sin_lo + dn * sin_hi


def _qkv_proj_kernel(h_ref, w_ref, cos_ref, slo_ref, shi_ref, o_ref, *, heads_per_tile):
    acc = jnp.dot(h_ref[...], w_ref[...], preferred_element_type=F32)
    cos_t, sin_lo, sin_hi = cos_ref[...], slo_ref[...], shi_ref[...]
    for hh in range(heads_per_tile):
        sl = slice(hh * HEAD_DIM, (hh + 1) * HEAD_DIM)
        o_ref[:, sl] = _rotary(acc[:, sl], cos_t, sin_lo, sin_hi)


def qkv_projection(h, w, col_tile0, qkv_width, seq, tm, tn):
    m, k = h.shape
    n_q_tiles = qkv_width // tn
    cos_t, sin_lo, sin_hi = rope_tables(seq)
    scale = HEAD_DIM ** -0.5
    zeros = jnp.zeros_like(cos_t)
    tables = (jnp.stack([cos_t * scale, cos_t, jnp.ones_like(cos_t)]),
              jnp.stack([sin_lo * scale, sin_lo, zeros]),
              jnp.stack([sin_hi * scale, sin_hi, zeros]))

    def tab_index(i, j):
        kind = jnp.where(j < n_q_tiles, 0, jnp.where(j < 2 * n_q_tiles, 1, 2))
        return (kind, i % (seq // tm), 0)

    tab_spec = pl.BlockSpec((None, tm, HEAD_DIM), tab_index)
    return pl.pallas_call(
        functools.partial(_qkv_proj_kernel, heads_per_tile=tn // HEAD_DIM),
        grid=(m // tm, 3 * n_q_tiles),
        in_specs=[pl.BlockSpec((tm, k), lambda i, j: (i, 0)),
                  pl.BlockSpec((k, tn), lambda i, j: (0, col_tile0 + j)),
                  tab_spec, tab_spec, tab_spec],
        out_specs=pl.BlockSpec((tm, tn), lambda i, j: (i, j)),
        out_shape=jax.ShapeDtypeStruct((m, 3 * qkv_width), F32),
        compiler_params=_params(2),
        name="qkv_proj",
    )(h, w, *tables)


def _gelu_tanh(x):
    c = math.sqrt(2.0 / math.pi)
    return 0.5 * x * (1.0 + jnp.tanh(c * (x + 0.044715 * (x * x * x))))


def _s5_kernel(u_ref, bd_ref, wc_ref, vc_ref, a_ref, o_ref, xf_scr, bc_scr, sin_scr, m_scr, w_scr, vt_scr,
               *, rows):
    t_ = S5_CHUNK
    half = bc_scr.shape[1] // 2
    n_pair = half // LANES

    @pl.when(pl.program_id(1) == 0)
    def _():
        zeros = jnp.zeros((LANES, LANES), BF16)
        for tin in range(t_):
            for tout in range(t_):
                blk = bd_ref[tout - tin] if tout >= tin else zeros
                m_scr[tin * LANES:(tin + 1) * LANES, tout * LANES:(tout + 1) * LANES] = blk
        kdim = m_scr.shape[0]
        row_group = (lax.broadcasted_iota(jnp.int32, (kdim, LANES), 0) // (LANES // S5_SLAB_GROUPS)) % S5_SLAB_GROUPS
        lane_half = lax.broadcasted_iota(jnp.int32, (kdim, LANES), 1) // (LANES // 2)
        diff = row_group - lane_half
        for ri in range(2):
            wv = wc_ref[ri]
            vv = vc_ref[ri]
            for k in range(n_pair):
                sel = diff == 2 * k
                cols = slice(ri * half + k * LANES, ri * half + (k + 1) * LANES)
                w_scr[:, cols] = jnp.where(sel, wv, 0.0).astype(BF16)
                vt_scr[:, cols] = jnp.where(sel, vv, 0.0).astype(BF16)

    xf_scr[...] = u_ref[...].astype(F32)
    planes = [xf_scr[pl.ds(t, rows, stride=t_), :].astype(BF16) for t in range(t_)]
    u = jnp.concatenate(planes, axis=1)
    y_intra = jnp.dot(u, m_scr[...], preferred_element_type=F32)
    bc_scr[...] = jnp.dot(u, w_scr[...], preferred_element_type=F32)

    ar = a_ref[0:1, :]
    ai = a_ref[1:2, :]

    def tile_body(i, carry):
        sr, si = carry
        base = pl.multiple_of(i * 8, 8)
        tile = bc_scr[pl.ds(base, 8), :]
        rows_r, rows_i = [], []
        for r in range(8):
            rows_r.append(sr)
            rows_i.append(si)
            br = tile[r:r + 1, :half]
            bi = tile[r:r + 1, half:]
            sr, si = ar * sr - ai * si + br, ar * si + ai * sr + bi
        sin_scr[pl.ds(base, 8), :] = jnp.concatenate(
            [jnp.concatenate(rows_r, axis=0), jnp.concatenate(rows_i, axis=0)], axis=1)
        return sr, si

    zero = jnp.zeros((1, half), F32)
    lax.fori_loop(0, rows // 8, tile_body, (zero, zero))

    y_state = lax.dot_general(sin_scr[...].astype(BF16), vt_scr[...], (((1,), (1,)), ((), ())),
                              preferred_element_type=F32)
    y = _gelu_tanh(y_intra + y_state)
    for t in range(t_):
        xf_scr[pl.ds(t, rows, stride=t_), :] = y[:, t * LANES:(t + 1) * LANES]
    o_ref[...] = xf_scr[...].astype(o_ref.dtype)


def s5_prepare(log_dt, a_re, a_im, b_re, b_im, c_re, c_im, d_skip):
    hp = lax.Precision.HIGHEST
    g, p = a_re.shape
    c = b_re.shape[-1]
    t_ = S5_CHUNK
    gl = S5_SLAB_GROUPS
    ns = g // gl
    assert gl * c == LANES and 2 * p == LANES
    a_re, a_im = a_re.astype(F32), a_im.astype(F32)
    dt = jnp.exp(log_dt.astype(F32))[:, None]
    ks = jnp.arange(t_ + 1, dtype=F32)[:, None, None]
    mag = jnp.exp(ks * (dt * a_re)[None])
    ang = ks * (dt * a_im)[None]
    pr, pi = mag * jnp.cos(ang), mag * jnp.sin(ang)
    lb_re, lb_im = pr[1], pi[1]
    den = a_re * a_re + a_im * a_im
    f_re = ((lb_re - 1.0) * a_re + lb_im * a_im) / den
    f_im = (lb_im * a_re - (lb_re - 1.0) * a_im) / den
    br = jnp.swapaxes(b_re.astype(F32), 1, 2)
    bi = jnp.swapaxes(b_im.astype(F32), 1, 2)
    bb_re = f_re[:, None, :] * br - f_im[:, None, :] * bi
    bb_im = f_re[:, None, :] * bi + f_im[:, None, :] * br
    cr, ci = c_re.astype(F32), c_im.astype(F32)
    cp_re = cr[None] * pr[:, :, None, :] - ci[None] * pi[:, :, None, :]
    cp_im = cr[None] * pi[:, :, None, :] + ci[None] * pr[:, :, None, :]
    kk = (jnp.einsum('kgop,gip->kgio', cp_re[:t_], bb_re, precision=hp)
          - jnp.einsum('kgop,gip->kgio', cp_im[:t_], bb_im, precision=hp))
    kk = kk.at[0].add(jnp.eye(c, dtype=F32)[None] * d_skip.astype(F32)[:, :, None])
    kx = kk.reshape(t_, ns, gl * c, c).transpose(1, 0, 2, 3)
    bd = jnp.tile(kx, (1, 1, 1, gl))
    same_group = (jnp.arange(LANES)[:, None] // c) == (jnp.arange(LANES)[None, :] // c)
    bd = jnp.where(same_group, bd, 0.0).astype(BF16)
    prr = pr[t_ - 1::-1][:t_]
    pir = pi[t_ - 1::-1][:t_]
    w_re = prr[:, :, None, :] * bb_re[None] - pir[:, :, None, :] * bb_im[None]
    w_im = prr[:, :, None, :] * bb_im[None] + pir[:, :, None, :] * bb_re[None]

    def compact(x):
        x = x.reshape(t_, ns, gl * c, p).transpose(1, 0, 2, 3).reshape(ns, t_ * gl * c, p)
        return jnp.concatenate([x, x], axis=-1)

    wc = jnp.stack([compact(w_re), compact(w_im)], axis=1)
    vc = jnp.stack([compact(cp_re[1:]), compact(-cp_im[1:])], axis=1)
    a_mat = jnp.stack([pr[t_], pi[t_]], axis=0).reshape(2, ns, gl * p).transpose(1, 0, 2)
    return bd, wc.astype(F32), vc.astype(F32), a_mat.astype(F32)


def s5_mixer_gelu(proj, col_block0, n_batch, seq, mats):
    bd, wc, vc, a_mat = mats
    ns = bd.shape[0]
    kdim = wc.shape[2]
    sdim = 2 * a_mat.shape[2]
    rows = seq // S5_CHUNK
    return pl.pallas_call(
        functools.partial(_s5_kernel, rows=rows),
        grid=(ns, n_batch),
        in_specs=[pl.BlockSpec((seq, LANES), lambda s, b: (b, col_block0 + s)),
                  pl.BlockSpec((None, S5_CHUNK, LANES, LANES), lambda s, b: (s, 0, 0, 0)),
                  pl.BlockSpec((None, 2, kdim, LANES), lambda s, b: (s, 0, 0, 0)),
                  pl.BlockSpec((None, 2, kdim, LANES), lambda s, b: (s, 0, 0, 0)),
                  pl.BlockSpec((None, 2, sdim // 2), lambda s, b: (s, 0, 0))],
        out_specs=pl.BlockSpec((seq, LANES), lambda s, b: (b, s)),
        out_shape=jax.ShapeDtypeStruct((n_batch * seq, ns * LANES), BF16),
        scratch_shapes=[pltpu.VMEM((seq, LANES), F32),
                        pltpu.VMEM((rows, sdim), F32),
                        pltpu.VMEM((rows, sdim), F32),
                        pltpu.VMEM((kdim, kdim), BF16),
                        pltpu.VMEM((kdim, sdim), BF16),
                        pltpu.VMEM((kdim, sdim), BF16)],
        compiler_params=_params(2),
        name="s5_mixer",
    )(proj, bd, wc, vc, a_mat)


def _attn_kernel(q_ref, k_ref, v_ref, o_ref, acc_scr, m_scr, l_scr, *, seq):
    g = pl.program_id(2)
    n_g = len(DILATION_PATTERNS)
    blk = ATTN_BLOCK
    row = lax.broadcasted_iota(jnp.int32, (blk, blk), 0)
    col = lax.broadcasted_iota(jnp.int32, (blk, blk), 1)
    cur_ok = col <= row
    prev_ok = col >= row
    nt = (((1,), (1,)), ((), ()))
    ones = jnp.ones((blk, HEAD_DIM), BF16)

    def run_group(gi, dil):
        first = gi == 0
        n_qb = seq // dil // blk

        def load(ref, start):
            if dil == 1:
                return ref[pl.ds(start, blk), :]
            return ref[pl.ds(start, blk, stride=dil), :]

        def store(ref, start, val):
            if dil == 1:
                ref[pl.ds(start, blk), :] = val
            else:
                ref[pl.ds(start, blk, stride=dil), :] = val

        def block(base, prev, prev_cond):
            q = load(q_ref, base).astype(BF16)
            kc = load(k_ref, base).astype(BF16)
            vc = jnp.concatenate([load(v_ref, base).astype(BF16), ones], axis=1)
            s_c = jnp.where(cur_ok, lax.dot_general(q, kc, nt, preferred_element_type=F32), NEG_BIG)
            m_blk = jnp.max(s_c, axis=1, keepdims=True)
            if prev is not None:
                ok = prev_ok if prev_cond is None else jnp.logical_and(prev_ok, prev_cond)
                s_p = jnp.where(ok, lax.dot_general(q, prev[0], nt, preferred_element_type=F32), NEG_BIG)
                m_blk = jnp.maximum(m_blk, jnp.max(s_p, axis=1, keepdims=True))
            if first:
                m_new = m_blk
            else:
                m_old = load(m_scr, base)
                m_new = jnp.maximum(m_old, m_blk)
            pv = jnp.dot(jnp.exp(s_c - m_new).astype(BF16), vc, preferred_element_type=F32)
            if prev is not None:
                pv = pv + jnp.dot(jnp.exp(s_p - m_new).astype(BF16), prev[1], preferred_element_type=F32)
            acc_new = pv[:, :HEAD_DIM]
            l_new = pv[:, HEAD_DIM:]
            if not first:
                alpha = jnp.exp(m_old - m_new)
                acc_new = alpha * load(acc_scr, base) + acc_new
                l_new = alpha * load(l_scr, base) + l_new
            store(acc_scr, base, acc_new)
            store(l_scr, base, l_new)
            if gi != n_g - 1:
                store(m_scr, base, jnp.broadcast_to(m_new, (blk, blk)))
            return kc, vc

        if n_qb >= ATTN_UNROLL:
            assert n_qb % ATTN_UNROLL == 0

            def res_body(res, _):
                def body(it, carry):
                    prev = carry
                    for u in range(ATTN_UNROLL):
                        base = (it * ATTN_UNROLL + u) * (blk * dil) + res
                        prev = block(base, prev, (it > 0) if u == 0 else None)
                    return prev
                init = (jnp.zeros((blk, HEAD_DIM), BF16), jnp.zeros((blk, 2 * HEAD_DIM), BF16))
                lax.fori_loop(0, n_qb // ATTN_UNROLL, body, init)
                return 0
            lax.fori_loop(0, dil, res_body, 0)
        else:
            assert ATTN_UNROLL % n_qb == 0
            res_per_body = ATTN_UNROLL // n_qb
            assert dil % res_per_body == 0

            def body(it, _):
                for rr in range(res_per_body):
                    res = it * res_per_body + rr
                    prev = None
                    for u in range(n_qb):
                        prev = block(u * (blk * dil) + res, prev, None)
                return 0
            lax.fori_loop(0, dil // res_per_body, body, 0)

    for gi, (window, dil) in enumerate(DILATION_PATTERNS):
        assert window // dil == blk
        pl.when(g == gi)(functools.partial(run_group, gi, dil))

    @pl.when(g == n_g - 1)
    def _():
        o_ref[...] = (acc_scr[...] / l_scr[...]).astype(o_ref.dtype)


def dilated_attention(qkv, n_batch, seq, n_heads):
    n_g = len(DILATION_PATTERNS)
    width = n_g * n_heads

    def col_spec(part):
        return pl.BlockSpec((seq, LANES), lambda b, h, g: (b, part * width + g * n_heads + h))

    return pl.pallas_call(
        functools.partial(_attn_kernel, seq=seq),
        grid=(n_batch, n_heads, n_g),
        in_specs=[col_spec(0), col_spec(1), col_spec(2)],
        out_specs=pl.BlockSpec((seq, LANES), lambda b, h, g: (b, h)),
        out_shape=jax.ShapeDtypeStruct((n_batch * seq, n_heads * LANES), BF16),
        scratch_shapes=[pltpu.VMEM((seq, LANES), F32)] * 3,
        compiler_params=_params(3),
        name="dilated_attention",
    )(qkv, qkv, qkv)


def _sigmoid(x):
    return 1.0 / (1.0 + jnp.exp(-x))


def _mix_kernel(y_ref, at_ref, wa_ref, wb_ref, wo_ref, gs_ref, ga_ref, o_ref):
    y = y_ref[...]
    glu_a = jnp.dot(y, wa_ref[...], preferred_element_type=F32)
    glu_b = jnp.dot(y, wb_ref[...], preferred_element_type=F32)
    attn = jnp.dot(at_ref[...], wo_ref[...], preferred_element_type=F32)
    ssm = glu_a * _sigmoid(glu_b)
    mixed = _sigmoid(gs_ref[...].astype(F32)) * ssm + _sigmoid(ga_ref[...].astype(F32)) * attn
    o_ref[...] = mixed.astype(o_ref.dtype)


def gated_merge(y, attn, w_glu, w_ao, proj, col_gs, col_ga, tm, tn):
    m, ky = y.shape
    ka = attn.shape[1]
    d = w_ao.shape[1]
    nj = d // tn
    return pl.pallas_call(
        _mix_kernel,
        grid=(m // tm, nj),
        in_specs=[pl.BlockSpec((tm, ky), lambda i, j: (i, 0)),
                  pl.BlockSpec((tm, ka), lambda i, j: (i, 0)),
                  pl.BlockSpec((ky, tn), lambda i, j: (0, j)),
                  pl.BlockSpec((ky, tn), lambda i, j: (0, nj + j)),
                  pl.BlockSpec((ka, tn), lambda i, j: (0, j)),
                  pl.BlockSpec((tm, tn), lambda i, j: (i, col_gs // tn + j)),
                  pl.BlockSpec((tm, tn), lambda i, j: (i, col_ga // tn + j))],
        out_specs=pl.BlockSpec((tm, tn), lambda i, j: (i, j)),
        out_shape=jax.ShapeDtypeStruct((m, d), BF16),
        compiler_params=_params(2),
        name="gated_merge",
    )(y, attn, w_glu, w_glu, w_ao, proj, proj)


def _ffn_up_kernel(h_ref, wa_ref, wg_ref, ca_ref, cg_ref, o_ref, ua_scr, ug_scr, halo_a, halo_g,
                   *, tm, blocks_per_seq):
    i = pl.program_id(0)
    j = pl.program_id(1)
    h = h_ref[...]
    seq_start = (i % blocks_per_seq) == 0

    def conv(w_ref, c_ref, u_scr, halo):
        u_scr[pl.ds(HALO_ROWS, tm), :] = jnp.dot(h, w_ref[...], preferred_element_type=F32)
        u_scr[pl.ds(0, HALO_ROWS), :] = jnp.where(seq_start, 0.0, halo[j])
        halo[j] = u_scr[pl.ds(tm, HALO_ROWS), :]
        cw = c_ref[...]
        return (cw[2:3, :] * u_scr[pl.ds(HALO_ROWS, tm), :]
                + cw[1:2, :] * u_scr[pl.ds(HALO_ROWS - 1, tm), :]
                + cw[0:1, :] * u_scr[pl.ds(HALO_ROWS - 2, tm), :]
                + cw[3:4, :])

    a = conv(wa_ref, ca_ref, ua_scr, halo_a)
    gv = conv(wg_ref, cg_ref, ug_scr, halo_g)
    o_ref[...] = (a * _sigmoid(a) * gv).astype(o_ref.dtype)


def ffn_up(h, wa, wg, ca, cg, seq, tm, tn):
    m, k = h.shape
    f = wa.shape[1]
    nj = f // tn
    return pl.pallas_call(
        functools.partial(_ffn_up_kernel, tm=tm, blocks_per_seq=seq // tm),
        grid=(m // tm, nj),
        in_specs=[pl.BlockSpec((tm, k), lambda i, j: (i, 0)),
                  pl.BlockSpec((k, tn), lambda i, j: (0, j)),
                  pl.BlockSpec((k, tn), lambda i, j: (0, j)),
                  pl.BlockSpec((HALO_ROWS, tn), lambda i, j: (0, j)),
                  pl.BlockSpec((HALO_ROWS, tn), lambda i, j: (0, j))],
        out_specs=pl.BlockSpec((tm, tn), lambda i, j: (i, j)),
        out_shape=jax.ShapeDtypeStruct((m, f), BF16),
        scratch_shapes=[pltpu.VMEM((tm + HALO_ROWS, tn), F32),
                        pltpu.VMEM((tm + HALO_ROWS, tn), F32),
                        pltpu.VMEM((nj, HALO_ROWS, tn), F32),
                        pltpu.VMEM((nj, HALO_ROWS, tn), F32)],
        compiler_params=_params(2),
        name="ffn_up",
    )(h, wa, wg, ca, cg)


def _pad_cols(w, n):
    return jnp.pad(w, ((0, 0), (0, n - w.shape[1])))


def kernel(x, g_mix, w_in, ssm_log_dt, ssm_a_re, ssm_a_im, ssm_b_re, ssm_b_im, ssm_c_re, ssm_c_im,
           ssm_d, w_glu, w_attn_out, w_out, g_ffn, w_up, conv_w, conv_b, w_down, g_final):
    b, l, d = x.shape
    depth = g_mix.shape[0]
    ssm_width = ssm_a_re.shape[1] * ssm_b_re.shape[-1]
    attn_width = w_attn_out.shape[1]
    n_heads = attn_width // HEAD_DIM
    qkv_width = len(DILATION_PATTERNS) * attn_width
    d_ff = w_down.shape[1]
    ff_tile = 512
    d_ff_p = -(-d_ff // ff_tile) * ff_tile
    proj_tile = 1024
    ssm_tiles = ssm_width // proj_tile
    qkv_tiles = 3 * qkv_width // proj_tile
    gate_tiles = 2 * d // proj_tile

    xf = x.reshape(b * l, d).astype(F32)
    for i in range(depth):
        h = rmsnorm(xf, g_mix[i], BF16)
        w_in_b = w_in[i].astype(BF16)
        proj = matmul(h, w_in_b, BF16, 1024, proj_tile, ssm_tiles + gate_tiles,
                      lambda j: jnp.where(j < ssm_tiles, j, j + qkv_tiles), "in_proj")
        qkv = qkv_projection(h, w_in_b, ssm_tiles, qkv_width, l, 1024, proj_tile)
        mats = s5_prepare(ssm_log_dt[i], ssm_a_re[i], ssm_a_im[i], ssm_b_re[i], ssm_b_im[i],
                          ssm_c_re[i], ssm_c_im[i], ssm_d[i])
        y = s5_mixer_gelu(proj, 0, b, l, mats)
        attn = dilated_attention(qkv, b, l, n_heads)
        mixed = gated_merge(y, attn, w_glu[i].astype(BF16), w_attn_out[i].astype(BF16), proj,
                            ssm_width, ssm_width + d, 1024, 512)
        xf = matmul_residual(mixed, w_out[i].astype(BF16), xf, 1024, 512, 1, "out_proj")

        h = rmsnorm(xf, g_ffn[i], BF16)
        wa = _pad_cols(w_up[i][:, :d_ff], d_ff_p).astype(BF16)
        wg = _pad_cols(w_up[i][:, d_ff:], d_ff_p).astype(BF16)
        zrow = jnp.zeros((HALO_ROWS - CONV_WIDTH - 1, d_ff_p), F32)
        ca = jnp.concatenate([_pad_cols(conv_w[i][:, :d_ff], d_ff_p),
                              _pad_cols(conv_b[i][None, :d_ff], d_ff_p), zrow], axis=0).astype(F32)
        cg = jnp.concatenate([_pad_cols(conv_w[i][:, d_ff:], d_ff_p),
                              _pad_cols(conv_b[i][None, d_ff:], d_ff_p), zrow], axis=0).astype(F32)
        act = ffn_up(h, wa, wg, ca, cg, l, 1024, ff_tile)
        wd = jnp.pad(w_down[i], ((0, d_ff_p - d_ff), (0, 0))).astype(BF16)
        xf = matmul_residual(act, wd, xf, 1024, 512, 2, "ffn_down")
    out = rmsnorm(xf, g_final, x.dtype)
    return out.reshape(b, l, d)
```

```python
import functools
import math

import jax
import jax.numpy as jnp
from jax import lax
from jax.experimental import pallas as pl
from jax.experimental.pallas import tpu as pltpu

F32 = jnp.float32
BF16 = jnp.bfloat16

RMS_EPS = 1e-5
LANES = 128
HEAD_DIM = 128
ROPE_DIM = HEAD_DIM // 4
ROPE_THETA = 500000.0
DILATION_PATTERNS = ((128, 1), (512, 4), (2048, 16))
ATTN_BLOCK = 128
ATTN_UNROLL = 8
ATTN_ORDER = tuple(sorted(range(len(DILATION_PATTERNS)), key=lambda i: -DILATION_PATTERNS[i][1]))
S5_CHUNK = 16
S5_SLAB_GROUPS = 8
CONV_WIDTH = 3
HALO_ROWS = 8
NEG_BIG = -1e30
VMEM_LIMIT = 56 * 1024 * 1024


def _params(n_axes):
    return pltpu.CompilerParams(dimension_semantics=("arbitrary",) * n_axes,
                                vmem_limit_bytes=VMEM_LIMIT)


def _rmsnorm_kernel(x_ref, g_ref, o_ref):
    x = x_ref[...]
    y = x * lax.rsqrt(jnp.mean(x * x, axis=-1, keepdims=True) + RMS_EPS)
    o_ref[...] = (y * g_ref[...]).astype(o_ref.dtype)


def rmsnorm(x2d, g, out_dtype, tm=256):
    m, d = x2d.shape
    return pl.pallas_call(
        _rmsnorm_kernel,
        grid=(m // tm,),
        in_specs=[pl.BlockSpec((tm, d), lambda i: (i, 0)),
                  pl.BlockSpec((1, d), lambda i: (0, 0))],
        out_specs=pl.BlockSpec((tm, d), lambda i: (i, 0)),
        out_shape=jax.ShapeDtypeStruct((m, d), out_dtype),
        compiler_params=_params(1),
        name="rmsnorm",
    )(x2d, g.reshape(1, d).astype(F32))


def _mm_kernel(a_ref, w_ref, o_ref):
    o_ref[...] = jnp.dot(a_ref[...], w_ref[...], preferred_element_type=F32).astype(o_ref.dtype)


def matmul(a, w, out_dtype, tm, tn, n_col_tiles, w_col_tile, name):
    m, k = a.shape
    return pl.pallas_call(
        _mm_kernel,
        grid=(m // tm, n_col_tiles),
        in_specs=[pl.BlockSpec((tm, k), lambda i, j: (i, 0)),
                  pl.BlockSpec((k, tn), lambda i, j: (0, w_col_tile(j)))],
        out_specs=pl.BlockSpec((tm, tn), lambda i, j: (i, j)),
        out_shape=jax.ShapeDtypeStruct((m, n_col_tiles * tn), out_dtype),
        compiler_params=_params(2),
        name=name,
    )(a, w)


def _mm_res_kernel(a_ref, w_ref, r_ref, o_ref):
    k = pl.program_id(2)
    part = jnp.dot(a_ref[...], w_ref[...], preferred_element_type=F32)

    @pl.when(k == 0)
    def _():
        o_ref[...] = r_ref[...] + part

    @pl.when(k != 0)
    def _():
        o_ref[...] += part


def matmul_residual(a, w, res, tm, tn, nk, name):
    m, k = a.shape
    n = w.shape[1]
    tk = k // nk
    return pl.pallas_call(
        _mm_res_kernel,
        grid=(m // tm, n // tn, nk),
        in_specs=[pl.BlockSpec((tm, tk), lambda i, j, kk: (i, kk)),
                  pl.BlockSpec((tk, tn), lambda i, j, kk: (kk, j)),
                  pl.BlockSpec((tm, tn), lambda i, j, kk: (i, j))],
        out_specs=pl.BlockSpec((tm, tn), lambda i, j, kk: (i, j)),
        out_shape=jax.ShapeDtypeStruct((m, n), F32),
        compiler_params=_params(3),
        name=name,
    )(a, w, res)


def rope_tables(seq):
    half = ROPE_DIM // 2
    inv_freq = ROPE_THETA ** (-jnp.arange(0, ROPE_DIM, 2, dtype=F32) / ROPE_DIM)
    ang = jnp.arange(seq, dtype=F32)[:, None] * inv_freq[None, :]
    cos, sin = jnp.cos(ang), jnp.sin(ang)
    pad = HEAD_DIM - ROPE_DIM
    cos_t = jnp.concatenate([cos, cos, jnp.ones((seq, pad), F32)], axis=1)
    zeros_h = jnp.zeros((seq, half), F32)
    zeros_p = jnp.zeros((seq, pad), F32)
    sin_lo = jnp.concatenate([-sin, zeros_h, zeros_p], axis=1)
    sin_hi = jnp.concatenate([zeros_h, sin, zeros_p], axis=1)
    return cos_t, sin_lo, sin_hi


def _rotary(x, cos_t, sin_lo, sin_hi):
    half = ROPE_DIM // 2
    up = pltpu.roll(x, HEAD_DIM - half, axis=1)
    dn = pltpu.roll(x, half, axis=1)
    return x * cos_t + up * sin_lo + dn * sin_hi


def _qkv_proj_kernel(h_ref, w_ref, cos_ref, slo_ref, shi_ref, o_ref, *, heads_per_tile):
    acc = jnp.dot(h_ref[...], w_ref[...], preferred_element_type=F32)
    cos_t, sin_lo, sin_hi = cos_ref[...], slo_ref[...], shi_ref[...]
    for hh in range(heads_per_tile):
        sl = slice(hh * HEAD_DIM, (hh + 1) * HEAD_DIM)
        o_ref[:, sl] = _rotary(acc[:, sl], cos_t, sin_lo, sin_hi)


def qkv_projection(h, w, col_tile0, qkv_width, seq, tm, tn):
    m, k = h.shape
    n_q_tiles = qkv_width // tn
    cos_t, sin_lo, sin_hi = rope_tables(seq)
    scale = HEAD_DIM ** -0.5
    zeros = jnp.zeros_like(cos_t)
    tables = (jnp.stack([cos_t * scale, cos_t, jnp.ones_like(cos_t)]),
              jnp.stack([sin_lo * scale, sin_lo, zeros]),
              jnp.stack([sin_hi * scale, sin_hi, zeros]))

    def tab_index(i, j):
        kind = jnp.where(j < n_q_tiles, 0, jnp.where(j < 2 * n_q_tiles, 1, 2))
        return (kind, i % (seq // tm), 0)

    tab_spec = pl.BlockSpec((None, tm, HEAD_DIM), tab_index)
    return pl.pallas_call(
        functools.partial(_qkv_proj_kernel, heads_per_tile=tn // HEAD_DIM),
        grid=(m // tm, 3 * n_q_tiles),
        in_specs=[pl.BlockSpec((tm, k), lambda i, j: (i, 0)),
                  pl.BlockSpec((k, tn), lambda i, j: (0, col_tile0 + j)),
                  tab_spec, tab_spec, tab_spec],
        out_specs=pl.BlockSpec((tm, tn), lambda i, j: (i, j)),
        out_shape=jax.ShapeDtypeStruct((m, 3 * qkv_width), F32),
        compiler_params=_params(2),
        name="qkv_proj",
    )(h, w, *tables)


def _gelu_tanh(x):
    c = math.sqrt(2.0 / math.pi)
    return 0.5 * x * (1.0 + jnp.tanh(c * (x + 0.044715 * (x * x * x))))


def _split_bf16(x):
    hi = x.astype(BF16)
    return hi, (x - hi.astype(F32)).astype(BF16)


def _s5_kernel(u_ref, bbn_ref, cl_ref, dsk_ref, wc_ref, vc_ref, a_ref, o_ref,
               xf_scr, bc_scr, sin_scr, m_scr, w_scr, vt_scr, *, rows):
    t_ = S5_CHUNK
    half = bc_scr.shape[1] // 2
    n_pair = half // LANES
    nt = (((1,), (1,)), ((), ()))

    @pl.when(pl.program_id(1) == 0)
    def _():
        b_hi, b_lo = _split_bf16(bbn_ref[...])
        c_hi, c_lo = _split_bf16(cl_ref[...])
        kt = (lax.dot_general(b_hi, c_hi, nt, preferred_element_type=F32)
              + lax.dot_general(b_hi, c_lo, nt, preferred_element_type=F32)
              + lax.dot_general(b_lo, c_hi, nt, preferred_element_type=F32))
        brow = lax.broadcasted_iota(jnp.int32, (LANES, LANES), 0)
        bcol = lax.broadcasted_iota(jnp.int32, (LANES, LANES), 1)
        chan = LANES // S5_SLAB_GROUPS
        same_group = (brow // chan) == (bcol // chan)
        zeros = jnp.zeros((LANES, LANES), BF16)
        for lag in range(t_):
            blk = jnp.where(same_group, kt[:, lag * LANES:(lag + 1) * LANES], 0.0)
            if lag == 0:
                blk = blk + jnp.where(brow == bcol, dsk_ref[...], 0.0)
            blk = blk.astype(BF16)
            for tin in range(t_ - lag):
                tout = tin + lag
                m_scr[tin * LANES:(tin + 1) * LANES, tout * LANES:(tout + 1) * LANES] = blk
        for tin in range(1, t_):
            for tout in range(tin):
                m_scr[tin * LANES:(tin + 1) * LANES, tout * LANES:(tout + 1) * LANES] = zeros
        kdim = m_scr.shape[0]
        row_group = (lax.broadcasted_iota(jnp.int32, (kdim, LANES), 0) // (LANES // S5_SLAB_GROUPS)) % S5_SLAB_GROUPS
        lane_half = lax.broadcasted_iota(jnp.int32, (kdim, LANES), 1) // (LANES // 2)
        diff = row_group - lane_half
        for ri in range(2):
            wv = wc_ref[ri]
            vv = vc_ref[ri]
            for k in range(n_pair):
                sel = diff == 2 * k
                cols = slice(ri * half + k * LANES, ri * half + (k + 1) * LANES)
                w_scr[:, cols] = jnp.where(sel, wv, 0.0).astype(BF16)
                vt_scr[:, cols] = jnp.where(sel, vv, 0.0).astype(BF16)

    xf_scr[...] = u_ref[...].astype(F32)
    planes = [xf_scr[pl.ds(t, rows, stride=t_), :].astype(BF16) for t in range(t_)]
    u = jnp.concatenate(planes, axis=1)
    y_intra = jnp.dot(u, m_scr[...], preferred_element_type=F32)
    bc_scr[...] = jnp.dot(u, w_scr[...], preferred_element_type=F32)

    ar = a_ref[0:1, :]
    ai = a_ref[1:2, :]

    def tile_body(i, carry):
        sr, si = carry
        base = pl.multiple_of(i * 8, 8)
        tile = bc_scr[pl.ds(base, 8), :]
        rows_r, rows_i = [], []
        for r in range(8):
            rows_r.append(sr)
            rows_i.append(si)
            br = tile[r:r + 1, :half]
            bi = tile[r:r + 1, half:]
            sr, si = ar * sr - ai * si + br, ar * si + ai * sr + bi
        sin_scr[pl.ds(base, 8), :] = jnp.concatenate(
            [jnp.concatenate(rows_r, axis=0), jnp.concatenate(rows_i, axis=0)], axis=1)
        return sr, si

    zero = jnp.zeros((1, half), F32)
    lax.fori_loop(0, rows // 8, tile_body, (zero, zero))

    y_state = lax.dot_general(sin_scr[...].astype(BF16), vt_scr[...], nt, preferred_element_type=F32)
    y = _gelu_tanh(y_intra + y_state)
    for t in range(t_):
        xf_scr[pl.ds(t, rows, stride=t_), :] = y[:, t * LANES:(t + 1) * LANES]
    o_ref[...] = xf_scr[...].astype(o_ref.dtype)


def s5_prepare(log_dt, a_re, a_im, b_re, b_im, c_re, c_im, d_skip):
    g, p = a_re.shape
    c = b_re.shape[-1]
    t_ = S5_CHUNK
    gl = S5_SLAB_GROUPS
    ns = g // gl
    assert gl * c == LANES and 2 * p == LANES
    a_re, a_im = a_re.astype(F32), a_im.astype(F32)
    dt = jnp.exp(log_dt.astype(F32))[:, None]
    ks = jnp.arange(t_ + 1, dtype=F32)[:, None, None]
    mag = jnp.exp(ks * (dt * a_re)[None])
    ang = ks * (dt * a_im)[None]
    pr, pi = mag * jnp.cos(ang), mag * jnp.sin(ang)
    lb_re, lb_im = pr[1], pi[1]
    den = a_re * a_re + a_im * a_im
    f_re = ((lb_re - 1.0) * a_re + lb_im * a_im) / den
    f_im = (lb_im * a_re - (lb_re - 1.0) * a_im) / den
    br = jnp.swapaxes(b_re.astype(F32), 1, 2)
    bi = jnp.swapaxes(b_im.astype(F32), 1, 2)
    bb_re = f_re[:, None, :] * br - f_im[:, None, :] * bi
    bb_im = f_re[:, None, :] * bi + f_im[:, None, :] * br
    cr, ci = c_re.astype(F32), c_im.astype(F32)
    cp_re = cr[None] * pr[:, :, None, :] - ci[None] * pi[:, :, None, :]
    cp_im = cr[None] * pi[:, :, None, :] + ci[None] * pr[:, :, None, :]
    prr = pr[t_ - 1::-1][:t_]
    pir = pi[t_ - 1::-1][:t_]
    w_re = prr[:, :, None, :] * bb_re[None] - pir[:, :, None, :] * bb_im[None]
    w_im = prr[:, :, None, :] * bb_im[None] + pir[:, :, None, :] * bb_re[None]

    def rows_tgc(x):
        return x.reshape(t_, ns, gl * c, p).transpose(1, 0, 2, 3).reshape(ns, t_ * gl * c, p)

    def twice(x):
        return jnp.concatenate([x, x], axis=-1)

    bbn = jnp.concatenate([bb_re, -bb_im], axis=-1).reshape(ns, gl * c, 2 * p)
    cl = jnp.concatenate([rows_tgc(cp_re[:t_]), rows_tgc(cp_im[:t_])], axis=-1)
    dsk = d_skip.astype(F32).reshape(ns, 1, gl * c)
    wc = jnp.stack([twice(rows_tgc(w_re)), twice(rows_tgc(w_im))], axis=1)
    vc = jnp.stack([twice(rows_tgc(cp_re[1:])), twice(rows_tgc(-cp_im[1:]))], axis=1)
    a_mat = jnp.stack([pr[t_], pi[t_]], axis=0).reshape(2, ns, gl * p).transpose(1, 0, 2)
    return bbn, cl, dsk, wc, vc, a_mat


def s5_mixer_gelu(proj, col_block0, n_batch, seq, mats):
    bbn, cl, dsk, wc, vc, a_mat = mats
    ns = bbn.shape[0]
    kdim = wc.shape[2]
    sdim = 2 * a_mat.shape[2]
    rows = seq // S5_CHUNK
    return pl.pallas_call(
        functools.partial(_s5_kernel, rows=rows),
        grid=(ns, n_batch),
        in_specs=[pl.BlockSpec((seq, LANES), lambda s, b: (b, col_block0 + s)),
                  pl.BlockSpec((None, LANES, LANES), lambda s, b: (s, 0, 0)),
                  pl.BlockSpec((None, kdim, LANES), lambda s, b: (s, 0, 0)),
                  pl.BlockSpec((None, 1, LANES), lambda s, b: (s, 0, 0)),
                  pl.BlockSpec((None, 2, kdim, LANES), lambda s, b: (s, 0, 0, 0)),
                  pl.BlockSpec((None, 2, kdim, LANES), lambda s, b: (s, 0, 0, 0)),
                  pl.BlockSpec((None, 2, sdim // 2), lambda s, b: (s, 0, 0))],
        out_specs=pl.BlockSpec((seq, LANES), lambda s, b: (b, s)),
        out_shape=jax.ShapeDtypeStruct((n_batch * seq, ns * LANES), BF16),
        scratch_shapes=[pltpu.VMEM((seq, LANES), F32),
                        pltpu.VMEM((rows, sdim), F32),
                        pltpu.VMEM((rows, sdim), F32),
                        pltpu.VMEM((kdim, kdim), BF16),
                        pltpu.VMEM((kdim, sdim), BF16),
                        pltpu.VMEM((kdim, sdim), BF16)],
        compiler_params=_params(2),
        name="s5_mixer",
    )(proj, bbn, cl, dsk, wc, vc, a_mat)


def _attn_kernel(q_ref, k_ref, v_ref, o_ref, acc_scr, m_scr, l_scr, *, seq):
    step = pl.program_id(2)
    n_g = len(DILATION_PATTERNS)
    blk = ATTN_BLOCK
    row = lax.broadcasted_iota(jnp.int32, (blk, blk), 0)
    col = lax.broadcasted_iota(jnp.int32, (blk, blk), 1)
    cur_ok = col <= row
    prev_ok = col >= row
    nt = (((1,), (1,)), ((), ()))
    ones = jnp.ones((blk, HEAD_DIM), BF16)

    def run_group(step_i, dil):
        first = step_i == 0
        last = step_i == n_g - 1
        n_qb = seq // dil // blk

        def load(ref, start):
            if dil == 1:
                return ref[pl.ds(start, blk), :]
            return ref[pl.ds(start, blk, stride=dil), :]

        def store(ref, start, val):
            if dil == 1:
                ref[pl.ds(start, blk), :] = val
            else:
                ref[pl.ds(start, blk, stride=dil), :] = val

        def block(base, prev, prev_cond):
            q = load(q_ref, base).astype(BF16)
            kc = load(k_ref, base).astype(BF16)
            vc = jnp.concatenate([load(v_ref, base).astype(BF16), ones], axis=1)
            s_c = jnp.where(cur_ok, lax.dot_general(q, kc, nt, preferred_element_type=F32), NEG_BIG)
            m_blk = jnp.max(s_c, axis=1, keepdims=True)
            if prev is not None:
                ok = prev_ok if prev_cond is None else jnp.logical_and(prev_ok, prev_cond)
                s_p = jnp.where(ok, lax.dot_general(q, prev[0], nt, preferred_element_type=F32), NEG_BIG)
                m_blk = jnp.maximum(m_blk, jnp.max(s_p, axis=1, keepdims=True))
            if first:
                m_new = m_blk
            else:
                m_old = load(m_scr, base)
                m_new = jnp.maximum(m_old, m_blk)
            pv = jnp.dot(jnp.exp(s_c - m_new).astype(BF16), vc, preferred_element_type=F32)
            if prev is not None:
                pv = pv + jnp.dot(jnp.exp(s_p - m_new).astype(BF16), prev[1], preferred_element_type=F32)
            acc_new = pv[:, :HEAD_DIM]
            l_new = pv[:, HEAD_DIM:]
            if not first:
                alpha = jnp.exp(m_old - m_new)
                acc_new = alpha * load(acc_scr, base) + acc_new
                l_new = alpha * load(l_scr, base) + l_new
            if last and dil == 1:
                o_ref[pl.ds(pl.multiple_of(base, blk), blk), :] = (acc_new / l_new).astype(o_ref.dtype)
            else:
                store(acc_scr, base, acc_new)
                store(l_scr, base, l_new)
                if not last:
                    store(m_scr, base, jnp.broadcast_to(m_new, (blk, blk)))
            return kc, vc

        if n_qb >= ATTN_UNROLL:
            assert n_qb % ATTN_UNROLL == 0

            def res_body(res, _):
                def body(it, carry):
                    prev = carry
                    for u in range(ATTN_UNROLL):
                        base = (it * ATTN_UNROLL + u) * (blk * dil) + res
                        prev = block(base, prev, (it > 0) if u == 0 else None)
                    return prev
                init = (jnp.zeros((blk, HEAD_DIM), BF16), jnp.zeros((blk, 2 * HEAD_DIM), BF16))
                lax.fori_loop(0, n_qb // ATTN_UNROLL, body, init)
                return 0
            if dil == 1:
                res_body(0, 0)
            else:
                lax.fori_loop(0, dil, res_body, 0)
        else:
            assert ATTN_UNROLL % n_qb == 0
            res_per_body = ATTN_UNROLL // n_qb
            assert dil % res_per_body == 0

            def body(it, _):
                for rr in range(res_per_body):
                    res = it * res_per_body + rr
                    prev = None
                    for u in range(n_qb):
                        prev = block(u * (blk * dil) + res, prev, None)
                return 0
            lax.fori_loop(0, dil // res_per_body, body, 0)

    for step_i, gi in enumerate(ATTN_ORDER):
        window, dil = DILATION_PATTERNS[gi]
        assert window // dil == blk
        pl.when(step == step_i)(functools.partial(run_group, step_i, dil))

    if DILATION_PATTERNS[ATTN_ORDER[-1]][1] != 1:
        @pl.when(step == n_g - 1)
        def _():
            o_ref[...] = (acc_scr[...] / l_scr[...]).astype(o_ref.dtype)


def dilated_attention(qkv, n_batch, seq, n_heads):
    n_g = len(DILATION_PATTERNS)
    width = n_g * n_heads

    def group_of_step(s):
        gi = ATTN_ORDER[-1]
        for step_i in range(n_g - 2, -1, -1):
            gi = jnp.where(s == step_i, ATTN_ORDER[step_i], gi)
        return gi

    def col_spec(part):
        return pl.BlockSpec((seq, LANES), lambda b, h, s: (b, part * width + group_of_step(s) * n_heads + h))

    return pl.pallas_call(
        functools.partial(_attn_kernel, seq=seq),
        grid=(n_batch, n_heads, n_g),
        in_specs=[col_spec(0), col_spec(1), col_spec(2)],
        out_specs=pl.BlockSpec((seq, LANES), lambda b, h, g: (b, h)),
        out_shape=jax.ShapeDtypeStruct((n_batch * seq, n_heads * LANES), BF16),
        scratch_shapes=[pltpu.VMEM((seq, LANES), F32)] * 3,
        compiler_params=_params(3),
        name="dilated_attention",
    )(qkv, qkv, qkv)


def _sigmoid(x):
    return 1.0 / (1.0 + jnp.exp(-x))


def _mix_kernel(y_ref, at_ref, wa_ref, wb_ref, wo_ref, gs_ref, ga_ref, o_ref):
    y = y_ref[...]
    glu_a = jnp.dot(y, wa_ref[...], preferred_element_type=F32)
    glu_b = jnp.dot(y, wb_ref[...], preferred_element_type=F32)
    attn = jnp.dot(at_ref[...], wo_ref[...], preferred_element_type=F32)
    ssm = glu_a * _sigmoid(glu_b)
    mixed = _sigmoid(gs_ref[...].astype(F32)) * ssm + _sigmoid(ga_ref[...].astype(F32)) * attn
    o_ref[...] = mixed.astype(o_ref.dtype)


def gated_merge(y, attn, w_glu, w_ao, proj, col_gs, col_ga, tm, tn):
    m, ky = y.shape
    ka = attn.shape[1]
    d = w_ao.shape[1]
    nj = d // tn
    return pl.pallas_call(
        _mix_kernel,
        grid=(m // tm, nj),
        in_specs=[pl.BlockSpec((tm, ky), lambda i, j: (i, 0)),
                  pl.BlockSpec((tm, ka), lambda i, j: (i, 0)),
                  pl.BlockSpec((ky, tn), lambda i, j: (0, j)),
                  pl.BlockSpec((ky, tn), lambda i, j: (0, nj + j)),
                  pl.BlockSpec((ka, tn), lambda i, j: (0, j)),
                  pl.BlockSpec((tm, tn), lambda i, j: (i, col_gs // tn + j)),
                  pl.BlockSpec((tm, tn), lambda i, j: (i, col_ga // tn + j))],
        out_specs=pl.BlockSpec((tm, tn), lambda i, j: (i, j)),
        out_shape=jax.ShapeDtypeStruct((m, d), BF16),
        compiler_params=_params(2),
        name="gated_merge",
    )(y, attn, w_glu, w_glu, w_ao, proj, proj)


def _ffn_up_kernel(h_ref, wa_ref, wg_ref, ca_ref, cg_ref, o_ref, ua_scr, ug_scr, halo_a, halo_g,
                   *, tm, blocks_per_seq):
    i = pl.program_id(0)
    j = pl.program_id(1)
    h = h_ref[...]
    seq_start = (i % blocks_per_seq) == 0

    def conv(w_ref, c_ref, u_scr, halo):
        u_scr[pl.ds(HALO_ROWS, tm), :] = jnp.dot(h, w_ref[...].astype(BF16), preferred_element_type=F32)
        u_scr[pl.ds(0, HALO_ROWS), :] = jnp.where(seq_start, 0.0, halo[j])
        halo[j] = u_scr[pl.ds(tm, HALO_ROWS), :]
        cw = c_ref[...]
        return (cw[2:3, :] * u_scr[pl.ds(HALO_ROWS, tm), :]
                + cw[1:2, :] * u_scr[pl.ds(HALO_ROWS - 1, tm), :]
                + cw[0:1, :] * u_scr[pl.ds(HALO_ROWS - 2, tm), :]
                + cw[3:4, :])

    a = conv(wa_ref, ca_ref, ua_scr, halo_a)
    gv = conv(wg_ref, cg_ref, ug_scr, halo_g)
    o_ref[...] = (a * _sigmoid(a) * gv).astype(o_ref.dtype)


def ffn_up(h, w_up, conv_p, seq, tm, tn):
    m, k = h.shape
    f = w_up.shape[1] // 2
    nj = f // tn
    assert nj * tn == f
    u_scratch = pltpu.VMEM((tm + HALO_ROWS, tn), F32)
    halo_scratch = pltpu.VMEM((nj, HALO_ROWS, tn), F32)
    return pl.pallas_call(
        functools.partial(_ffn_up_kernel, tm=tm, blocks_per_seq=seq // tm),
        grid=(m // tm, nj),
        in_specs=[pl.BlockSpec((tm, k), lambda i, j: (i, 0)),
                  pl.BlockSpec((k, tn), lambda i, j: (0, j)),
                  pl.BlockSpec((k, tn), lambda i, j: (0, nj + j)),
                  pl.BlockSpec((HALO_ROWS, tn), lambda i, j: (0, j)),
                  pl.BlockSpec((HALO_ROWS, tn), lambda i, j: (0, nj + j))],
        out_specs=pl.BlockSpec((tm, tn), lambda i, j: (i, j)),
        out_shape=jax.ShapeDtypeStruct((m, f), BF16),
        scratch_shapes=[u_scratch, u_scratch, halo_scratch, halo_scratch],
        compiler_params=_params(2),
        name="ffn_up",
    )(h, w_up, w_up, conv_p, conv_p)


def kernel(x, g_mix, w_in, ssm_log_dt, ssm_a_re, ssm_a_im, ssm_b_re, ssm_b_im, ssm_c_re, ssm_c_im,
           ssm_d, w_glu, w_attn_out, w_out, g_ffn, w_up, conv_w, conv_b, w_down, g_final):
    b, l, d = x.shape
    depth = g_mix.shape[0]
    ssm_width = ssm_a_re.shape[1] * ssm_b_re.shape[-1]
    attn_width = w_attn_out.shape[1]
    n_heads = attn_width // HEAD_DIM
    qkv_width = len(DILATION_PATTERNS) * attn_width
    d_ff = w_down.shape[1]
    ff_tile = 256
    proj_tile = 1024
    ssm_tiles = ssm_width // proj_tile
    qkv_tiles = 3 * qkv_width // proj_tile
    gate_tiles = 2 * d // proj_tile

    xf = x.reshape(b * l, d).astype(F32)
    for i in range(depth):
        h = rmsnorm(xf, g_mix[i], BF16)
        w_in_b = w_in[i].astype(BF16)
        proj = matmul(h, w_in_b, BF16, 1024, proj_tile, ssm_tiles + gate_tiles,
                      lambda j: jnp.where(j < ssm_tiles, j, j + qkv_tiles), "in_proj")
        qkv = qkv_projection(h, w_in_b, ssm_tiles, qkv_width, l, 1024, proj_tile)
        mats = s5_prepare(ssm_log_dt[i], ssm_a_re[i], ssm_a_im[i], ssm_b_re[i], ssm_b_im[i],
                          ssm_c_re[i], ssm_c_im[i], ssm_d[i])
        y = s5_mixer_gelu(proj, 0, b, l, mats)
        attn = dilated_attention(qkv, b, l, n_heads)
        mixed = gated_merge(y, attn, w_glu[i].astype(BF16), w_attn_out[i].astype(BF16), proj,
                            ssm_width, ssm_width + d, 1024, 512)
        xf = matmul_residual(mixed, w_out[i].astype(BF16), xf, 1024, 512, 1, "out_proj")

        h = rmsnorm(xf, g_ffn[i], BF16)
        zrow = jnp.zeros((HALO_ROWS - CONV_WIDTH - 1, 2 * d_ff), F32)
        conv_p = jnp.concatenate([conv_w[i].astype(F32), conv_b[i][None].astype(F32), zrow], axis=0)
        act = ffn_up(h, w_up[i].astype(F32), conv_p, l, 1024, ff_tile)
        xf = matmul_residual(act, w_down[i].astype(BF16), xf, 512, 512, 1, "ffn_down")
    out = rmsnorm(xf, g_final, x.dtype)
    return out.reshape(b, l, d)
```

```python
import functools
import math

import jax
import jax.numpy as jnp
from jax import lax
from jax.experimental import pallas as pl
from jax.experimental.pallas import tpu as pltpu

F32 = jnp.float32
BF16 = jnp.bfloat16

RMS_EPS = 1e-5
LANES = 128
HEAD_DIM = 128
ROPE_DIM = HEAD_DIM // 4
ROPE_THETA = 500000.0
DILATION_PATTERNS = ((128, 1), (512, 4), (2048, 16))
ATTN_BLOCK = 128
ATTN_UNROLL = 8
ATTN_ORDER = tuple(sorted(range(len(DILATION_PATTERNS)), key=lambda i: -DILATION_PATTERNS[i][1]))
S5_CHUNK = 16
S5_SLAB_GROUPS = 8
CONV_WIDTH = 3
HALO_ROWS = 8
NEG_BIG = -1e30
VMEM_LIMIT = 56 * 1024 * 1024


def _params(n_axes):
    return pltpu.CompilerParams(dimension_semantics=("arbitrary",) * n_axes,
                                vmem_limit_bytes=VMEM_LIMIT)


def _rmsnorm_kernel(x_ref, g_ref, o_ref):
    x = x_ref[...]
    y = x * lax.rsqrt(jnp.mean(x * x, axis=-1, keepdims=True) + RMS_EPS)
    o_ref[...] = (y * g_ref[...]).astype(o_ref.dtype)


def rmsnorm(x2d, g, out_dtype, tm=256):
    m, d = x2d.shape
    return pl.pallas_call(
        _rmsnorm_kernel,
        grid=(m // tm,),
        in_specs=[pl.BlockSpec((tm, d), lambda i: (i, 0)),
                  pl.BlockSpec((1, d), lambda i: (0, 0))],
        out_specs=pl.BlockSpec((tm, d), lambda i: (i, 0)),
        out_shape=jax.ShapeDtypeStruct((m, d), out_dtype),
        compiler_params=_params(1),
        name="rmsnorm",
    )(x2d, g.reshape(1, d).astype(F32))


def _mm_kernel(a_ref, w_ref, o_ref):
    o_ref[...] = jnp.dot(a_ref[...], w_ref[...], preferred_element_type=F32).astype(o_ref.dtype)


def matmul(a, w, out_dtype, tm, tn, n_col_tiles, w_col_tile, name):
    m, k = a.shape
    return pl.pallas_call(
        _mm_kernel,
        grid=(m // tm, n_col_tiles),
        in_specs=[pl.BlockSpec((tm, k), lambda i, j: (i, 0)),
                  pl.BlockSpec((k, tn), lambda i, j: (0, w_col_tile(j)))],
        out_specs=pl.BlockSpec((tm, tn), lambda i, j: (i, j)),
        out_shape=jax.ShapeDtypeStruct((m, n_col_tiles * tn), out_dtype),
        compiler_params=_params(2),
        name=name,
    )(a, w)


def _mm_res_kernel(a_ref, w_ref, r_ref, o_ref):
    k = pl.program_id(2)
    part = jnp.dot(a_ref[...], w_ref[...], preferred_element_type=F32)

    @pl.when(k == 0)
    def _():
        o_ref[...] = r_ref[...] + part

    @pl.when(k != 0)
    def _():
        o_ref[...] += part


def matmul_residual(a, w, res, tm, tn, nk, name, a_single_buffer=False):
    m = a.shape[0]
    k, n = w.shape
    tk = k // nk
    a_mode = pl.Buffered(1) if a_single_buffer else None
    return pl.pallas_call(
        _mm_res_kernel,
        grid=(m // tm, n // tn, nk),
        in_specs=[pl.BlockSpec((tm, tk), lambda i, j, kk: (i, kk), pipeline_mode=a_mode),
                  pl.BlockSpec((tk, tn), lambda i, j, kk: (kk, j)),
                  pl.BlockSpec((tm, tn), lambda i, j, kk: (i, j))],
        out_specs=pl.BlockSpec((tm, tn), lambda i, j, kk: (i, j)),
        out_shape=jax.ShapeDtypeStruct((m, n), F32),
        compiler_params=_params(3),
        name=name,
    )(a, w, res)


def rope_tables(seq):
    half = ROPE_DIM // 2
    inv_freq = ROPE_THETA ** (-jnp.arange(0, ROPE_DIM, 2, dtype=F32) / ROPE_DIM)
    ang = jnp.arange(seq, dtype=F32)[:, None] * inv_freq[None, :]
    cos, sin = jnp.cos(ang), jnp.sin(ang)
    pad = HEAD_DIM - ROPE_DIM
    cos_t = jnp.concatenate([cos, cos, jnp.ones((seq, pad), F32)], axis=1)
    zeros_h = jnp.zeros((seq, half), F32)
    zeros_p = jnp.zeros((seq, pad), F32)
    sin_lo = jnp.concatenate([-sin, zeros_h, zeros_p], axis=1)
    sin_hi = jnp.concatenate([zeros_h, sin, zeros_p], axis=1)
    return cos_t, sin_lo, sin_hi


def _rotary(x, cos_t, sin_lo, sin_hi):
    half = ROPE_DIM // 2
    up = pltpu.roll(x, HEAD_DIM - half, axis=1)
    dn = pltpu.roll(x, half, axis=1)
    return x * cos_t + up * sin_lo + dn * sin_hi


def _qkv_proj_kernel(h_ref, w_ref, cos_ref, slo_ref, shi_ref, o_ref, *, heads_per_tile):
    acc = jnp.dot(h_ref[...], w_ref[...], preferred_element_type=F32)
    cos_t, sin_lo, sin_hi = cos_ref[...], slo_ref[...], shi_ref[...]
    for hh in range(heads_per_tile):
        sl = slice(hh * HEAD_DIM, (hh + 1) * HEAD_DIM)
        o_ref[:, sl] = _rotary(acc[:, sl], cos_t, sin_lo, sin_hi)


def qkv_projection(h, w, col_tile0, qkv_width, seq, tm, tn):
    m, k = h.shape
    n_q_tiles = qkv_width // tn
    cos_t, sin_lo, sin_hi = rope_tables(seq)
    scale = HEAD_DIM ** -0.5
    zeros = jnp.zeros_like(cos_t)
    tables = (jnp.stack([cos_t * scale, cos_t, jnp.ones_like(cos_t)]),
              jnp.stack([sin_lo * scale, sin_lo, zeros]),
              jnp.stack([sin_hi * scale, sin_hi, zeros]))

    def tab_index(i, j):
        kind = jnp.where(j < n_q_tiles, 0, jnp.where(j < 2 * n_q_tiles, 1, 2))
        return (kind, i % (seq // tm), 0)

    tab_spec = pl.BlockSpec((None, tm, HEAD_DIM), tab_index)
    return pl.pallas_call(
        functools.partial(_qkv_proj_kernel, heads_per_tile=tn // HEAD_DIM),
        grid=(m // tm, 3 * n_q_tiles),
        in_specs=[pl.BlockSpec((tm, k), lambda i, j: (i, 0)),
                  pl.BlockSpec((k, tn), lambda i, j: (0, col_tile0 + j)),
                  tab_spec, tab_spec, tab_spec],
        out_specs=pl.BlockSpec((tm, tn), lambda i, j: (i, j)),
        out_shape=jax.ShapeDtypeStruct((m, 3 * qkv_width), F32),
        compiler_params=_params(2),
        name="qkv_proj",
    )(h, w, *tables)


def _gelu_tanh(x):
    c = math.sqrt(2.0 / math.pi)
    return 0.5 * x * (1.0 + jnp.tanh(c * (x + 0.044715 * (x * x * x))))


def _split_bf16(x):
    hi = x.astype(BF16)
    return hi, (x - hi.astype(F32)).astype(BF16)


def _s5_kernel(u_ref, bbn_ref, cl_ref, dsk_ref, wc_ref, vc_ref, a_ref, o_ref,
               xf_scr, bc_scr, sin_scr, m_scr, w_scr, vt_scr, *, rows, n_seq):
    t_ = S5_CHUNK
    half = bc_scr.shape[1] // 2
    n_pair = half // LANES
    nt = (((1,), (1,)), ((), ()))

    @pl.when(pl.program_id(1) == 0)
    def _():
        b_hi, b_lo = _split_bf16(bbn_ref[...])
        c_hi, c_lo = _split_bf16(cl_ref[...])
        kt = (lax.dot_general(b_hi, c_hi, nt, preferred_element_type=F32)
              + lax.dot_general(b_hi, c_lo, nt, preferred_element_type=F32)
              + lax.dot_general(b_lo, c_hi, nt, preferred_element_type=F32))
        brow = lax.broadcasted_iota(jnp.int32, (LANES, LANES), 0)
        bcol = lax.broadcasted_iota(jnp.int32, (LANES, LANES), 1)
        chan = LANES // S5_SLAB_GROUPS
        same_group = (brow // chan) == (bcol // chan)
        zeros = jnp.zeros((LANES, LANES), BF16)
        for lag in range(t_):
            blk = jnp.where(same_group, kt[:, lag * LANES:(lag + 1) * LANES], 0.0)
            if lag == 0:
                blk = blk + jnp.where(brow == bcol, dsk_ref[...], 0.0)
            blk = blk.astype(BF16)
            for tin in range(t_ - lag):
                tout = tin + lag
                m_scr[tin * LANES:(tin + 1) * LANES, tout * LANES:(tout + 1) * LANES] = blk
        for tout in range(0, t_, 2):
            m_scr[(tout + 1) * LANES:(tout + 2) * LANES, tout * LANES:(tout + 1) * LANES] = zeros
        kdim = m_scr.shape[0]
        row_group = (lax.broadcasted_iota(jnp.int32, (kdim, LANES), 0) // (LANES // S5_SLAB_GROUPS)) % S5_SLAB_GROUPS
        lane_half = lax.broadcasted_iota(jnp.int32, (kdim, LANES), 1) // (LANES // 2)
        diff = row_group - lane_half
        for ri in range(2):
            wv = wc_ref[ri]
            vv = vc_ref[ri]
            for k in range(n_pair):
                sel = diff == 2 * k
                cols = slice(ri * half + k * LANES, ri * half + (k + 1) * LANES)
                w_scr[:, cols] = jnp.where(sel, wv, 0.0).astype(BF16)
                vt_scr[:, cols] = jnp.where(sel, vv, 0.0).astype(BF16)

    all_rows = rows * n_seq
    xf_scr[...] = u_ref[...].astype(F32)
    planes = [xf_scr[pl.ds(t, all_rows, stride=t_), :].astype(BF16) for t in range(t_)]
    u = jnp.concatenate(planes, axis=1)
    bc_scr[...] = jnp.dot(u, w_scr[...], preferred_element_type=F32)

    ar = a_ref[0:1, :]
    ai = a_ref[1:2, :]

    def tile_body(i, carry):
        new_carry = []
        for q in range(n_seq):
            sr, si = carry[q]
            base = pl.multiple_of(q * rows + i * 8, 8)
            tile = bc_scr[pl.ds(base, 8), :]
            rows_r, rows_i = [], []
            for r in range(8):
                rows_r.append(sr)
                rows_i.append(si)
                br = tile[r:r + 1, :half]
                bi = tile[r:r + 1, half:]
                sr, si = ar * sr - ai * si + br, ar * si + ai * sr + bi
            sin_scr[pl.ds(base, 8), :] = jnp.concatenate(
                [jnp.concatenate(rows_r, axis=0), jnp.concatenate(rows_i, axis=0)], axis=1)
            new_carry.append((sr, si))
        return tuple(new_carry)

    zero = jnp.zeros((1, half), F32)
    lax.fori_loop(0, rows // 8, tile_body, ((zero, zero),) * n_seq)

    s_in = sin_scr[...].astype(BF16)
    tile_w = 2 * LANES
    for jt in range(t_ // 2):
        k_hi = (2 * jt + 2) * LANES
        cols = slice(jt * tile_w, (jt + 1) * tile_w)
        y = (jnp.dot(u[:, :k_hi], m_scr[0:k_hi, cols], preferred_element_type=F32)
             + lax.dot_general(s_in, vt_scr[cols, :], nt, preferred_element_type=F32))
        y = _gelu_tanh(y)
        for tt in range(2):
            xf_scr[pl.ds(2 * jt + tt, all_rows, stride=t_), :] = y[:, tt * LANES:(tt + 1) * LANES]
    o_ref[...] = xf_scr[...].astype(o_ref.dtype)


def s5_prepare(log_dt, a_re, a_im, b_re, b_im, c_re, c_im, d_skip):
    g, p = a_re.shape
    c = b_re.shape[-1]
    t_ = S5_CHUNK
    gl = S5_SLAB_GROUPS
    ns = g // gl
    assert gl * c == LANES and 2 * p == LANES
    a_re, a_im = a_re.astype(F32), a_im.astype(F32)
    dt = jnp.exp(log_dt.astype(F32))[:, None]
    ks = jnp.arange(t_ + 1, dtype=F32)[:, None, None]
    mag = jnp.exp(ks * (dt * a_re)[None])
    ang = ks * (dt * a_im)[None]
    pr, pi = mag * jnp.cos(ang), mag * jnp.sin(ang)
    lb_re, lb_im = pr[1], pi[1]
    den = a_re * a_re + a_im * a_im
    f_re = ((lb_re - 1.0) * a_re + lb_im * a_im) / den
    f_im = (lb_im * a_re - (lb_re - 1.0) * a_im) / den
    br = jnp.swapaxes(b_re.astype(F32), 1, 2)
    bi = jnp.swapaxes(b_im.astype(F32), 1, 2)
    bb_re = f_re[:, None, :] * br - f_im[:, None, :] * bi
    bb_im = f_re[:, None, :] * bi + f_im[:, None, :] * br
    cr, ci = c_re.astype(F32), c_im.astype(F32)
    cp_re = cr[None] * pr[:, :, None, :] - ci[None] * pi[:, :, None, :]
    cp_im = cr[None] * pi[:, :, None, :] + ci[None] * pr[:, :, None, :]
    prr = pr[t_ - 1::-1][:t_]
    pir = pi[t_ - 1::-1][:t_]
    w_re = prr[:, :, None, :] * bb_re[None] - pir[:, :, None, :] * bb_im[None]
    w_im = prr[:, :, None, :] * bb_im[None] + pir[:, :, None, :] * bb_re[None]

    def rows_tgc(x):
        return x.reshape(t_, ns, gl * c, p).transpose(1, 0, 2, 3).reshape(ns, t_ * gl * c, p)

    def twice(x):
        return jnp.concatenate([x, x], axis=-1)

    bbn = jnp.concatenate([bb_re, -bb_im], axis=-1).reshape(ns, gl * c, 2 * p)
    cl = jnp.concatenate([rows_tgc(cp_re[:t_]), rows_tgc(cp_im[:t_])], axis=-1)
    dsk = d_skip.astype(F32).reshape(ns, 1, gl * c)
    wc = jnp.stack([twice(rows_tgc(w_re)), twice(rows_tgc(w_im))], axis=1)
    vc = jnp.stack([twice(rows_tgc(cp_re[1:])), twice(rows_tgc(-cp_im[1:]))], axis=1)
    a_mat = jnp.stack([pr[t_], pi[t_]], axis=0).reshape(2, ns, gl * p).transpose(1, 0, 2)
    return bbn, cl, dsk, wc, vc, a_mat


def s5_mixer_gelu(proj, col_block0, n_batch, seq, mats):
    bbn, cl, dsk, wc, vc, a_mat = mats
    ns = bbn.shape[0]
    kdim = wc.shape[2]
    sdim = 2 * a_mat.shape[2]
    rows = seq // S5_CHUNK
    n_seq = 2 if n_batch % 2 == 0 else 1
    return pl.pallas_call(
        functools.partial(_s5_kernel, rows=rows, n_seq=n_seq),
        grid=(ns, n_batch // n_seq),
        in_specs=[pl.BlockSpec((n_seq * seq, LANES), lambda s, b: (b, col_block0 + s)),
                  pl.BlockSpec((None, LANES, LANES), lambda s, b: (s, 0, 0)),
                  pl.BlockSpec((None, kdim, LANES), lambda s, b: (s, 0, 0)),
                  pl.BlockSpec((None, 1, LANES), lambda s, b: (s, 0, 0)),
                  pl.BlockSpec((None, 2, kdim, LANES), lambda s, b: (s, 0, 0, 0)),
                  pl.BlockSpec((None, 2, kdim, LANES), lambda s, b: (s, 0, 0, 0)),
                  pl.BlockSpec((None, 2, sdim // 2), lambda s, b: (s, 0, 0))],
        out_specs=pl.BlockSpec((n_seq * seq, LANES), lambda s, b: (b, s)),
        out_shape=jax.ShapeDtypeStruct((n_batch * seq, ns * LANES), BF16),
        scratch_shapes=[pltpu.VMEM((n_seq * seq, LANES), F32),
                        pltpu.VMEM((n_seq * rows, sdim), F32),
                        pltpu.VMEM((n_seq * rows, sdim), F32),
                        pltpu.VMEM((kdim, kdim), BF16),
                        pltpu.VMEM((kdim, sdim), BF16),
                        pltpu.VMEM((kdim, sdim), BF16)],
        compiler_params=_params(2),
        name="s5_mixer",
    )(proj, bbn, cl, dsk, wc, vc, a_mat)


def _attn_kernel(q_ref, k_ref, v_ref, o_ref, acc_scr, m_scr, l_scr, *, seq):
    step = pl.program_id(2)
    n_g = len(DILATION_PATTERNS)
    blk = ATTN_BLOCK
    row = lax.broadcasted_iota(jnp.int32, (blk, blk), 0)
    col = lax.broadcasted_iota(jnp.int32, (blk, blk), 1)
    cur_ok = col <= row
    prev_ok = col >= row
    nt = (((1,), (1,)), ((), ()))
    ones = jnp.ones((blk, HEAD_DIM), BF16)

    def run_group(step_i, dil):
        first = step_i == 0
        last = step_i == n_g - 1
        n_qb = seq // dil // blk

        def load(ref, start):
            if dil == 1:
                return ref[pl.ds(start, blk), :]
            return ref[pl.ds(start, blk, stride=dil), :]

        def store(ref, start, val):
            if dil == 1:
                ref[pl.ds(start, blk), :] = val
            else:
                ref[pl.ds(start, blk, stride=dil), :] = val

        def block(base, prev, prev_cond):
            q = load(q_ref, base).astype(BF16)
            kc = load(k_ref, base).astype(BF16)
            vc = jnp.concatenate([load(v_ref, base).astype(BF16), ones], axis=1)
            s_c = jnp.where(cur_ok, lax.dot_general(q, kc, nt, preferred_element_type=F32), NEG_BIG)
            m_blk = jnp.max(s_c, axis=1, keepdims=True)
            if prev is not None:
                ok = prev_ok if prev_cond is None else jnp.logical_and(prev_ok, prev_cond)
                s_p = jnp.where(ok, lax.dot_general(q, prev[0], nt, preferred_element_type=F32), NEG_BIG)
                m_blk = jnp.maximum(m_blk, jnp.max(s_p, axis=1, keepdims=True))
            if first:
                m_new = m_blk
            else:
                m_old = load(m_scr, base)
                m_new = jnp.maximum(m_old, m_blk)
            pv = jnp.dot(jnp.exp(s_c - m_new).astype(BF16), vc, preferred_element_type=F32)
            if prev is not None:
                pv = pv + jnp.dot(jnp.exp(s_p - m_new).astype(BF16), prev[1], preferred_element_type=F32)
            acc_new = pv[:, :HEAD_DIM]
            l_new = pv[:, HEAD_DIM:]
            if not first:
                alpha = jnp.exp(m_old - m_new)
                acc_new = alpha * load(acc_scr, base) + acc_new
                l_new = alpha * load(l_scr, base) + l_new
            if last and dil == 1:
                o_ref[pl.ds(pl.multiple_of(base, blk), blk), :] = (acc_new / l_new).astype(o_ref.dtype)
            else:
                store(acc_scr, base, acc_new)
                store(l_scr, base, l_new)
                if not last:
                    store(m_scr, base, jnp.broadcast_to(m_new, (blk, blk)))
            return kc, vc

        if n_qb >= ATTN_UNROLL:
            assert n_qb % ATTN_UNROLL == 0

            def res_body(res, _):
                def body(it, carry):
                    prev = carry
                    for u in range(ATTN_UNROLL):
                        base = (it * ATTN_UNROLL + u) * (blk * dil) + res
                        prev = block(base, prev, (it > 0) if u == 0 else None)
                    return prev
                init = (jnp.zeros((blk, HEAD_DIM), BF16), jnp.zeros((blk, 2 * HEAD_DIM), BF16))
                lax.fori_loop(0, n_qb // ATTN_UNROLL, body, init)
                return 0
            if dil == 1:
                res_body(0, 0)
            else:
                lax.fori_loop(0, dil, res_body, 0)
        else:
            assert ATTN_UNROLL % n_qb == 0
            res_per_body = ATTN_UNROLL // n_qb
            assert dil % res_per_body == 0

            def body(it, _):
                for rr in range(res_per_body):
                    res = it * res_per_body + rr
                    prev = None
                    for u in range(n_qb):
                        prev = block(u * (blk * dil) + res, prev, None)
                return 0
            lax.fori_loop(0, dil // res_per_body, body, 0)

    for step_i, gi in enumerate(ATTN_ORDER):
        window, dil = DILATION_PATTERNS[gi]
        assert window // dil == blk
        pl.when(step == step_i)(functools.partial(run_group, step_i, dil))

    if DILATION_PATTERNS[ATTN_ORDER[-1]][1] != 1:
        @pl.when(step == n_g - 1)
        def _():
            o_ref[...] = (acc_scr[...] / l_scr[...]).astype(o_ref.dtype)


def dilated_attention(qkv, n_batch, seq, n_heads):
    n_g = len(DILATION_PATTERNS)
    width = n_g * n_heads

    def group_of_step(s):
        gi = ATTN_ORDER[-1]
        for step_i in range(n_g - 2, -1, -1):
            gi = jnp.where(s == step_i, ATTN_ORDER[step_i], gi)
        return gi

    def col_spec(part):
        return pl.BlockSpec((seq, LANES), lambda b, h, s: (b, part * width + group_of_step(s) * n_heads + h))

    return pl.pallas_call(
        functools.partial(_attn_kernel, seq=seq),
        grid=(n_batch, n_heads, n_g),
        in_specs=[col_spec(0), col_spec(1), col_spec(2)],
        out_specs=pl.BlockSpec((seq, LANES), lambda b, h, g: (b, h)),
        out_shape=jax.ShapeDtypeStruct((n_batch * seq, n_heads * LANES), BF16),
        scratch_shapes=[pltpu.VMEM((seq, LANES), F32)] * 3,
        compiler_params=_params(3),
        name="dilated_attention",
    )(qkv, qkv, qkv)


def _sigmoid(x):
    return 1.0 / (1.0 + jnp.exp(-x))


def _mix_kernel(y_ref, at_ref, wa_ref, wb_ref, wo_ref, gs_ref, ga_ref, o_ref):
    y = y_ref[...]
    glu_a = jnp.dot(y, wa_ref[...], preferred_element_type=F32)
    glu_b = jnp.dot(y, wb_ref[...], preferred_element_type=F32)
    attn = jnp.dot(at_ref[...], wo_ref[...], preferred_element_type=F32)
    ssm = glu_a * _sigmoid(glu_b)
    mixed = _sigmoid(gs_ref[...].astype(F32)) * ssm + _sigmoid(ga_ref[...].astype(F32)) * attn
    o_ref[...] = mixed.astype(o_ref.dtype)


def gated_merge(y, attn, w_glu, w_ao, proj, col_gs, col_ga, tm, tn):
    m, ky = y.shape
    ka = attn.shape[1]
    d = w_ao.shape[1]
    nj = d // tn
    return pl.pallas_call(
        _mix_kernel,
        grid=(m // tm, nj),
        in_specs=[pl.BlockSpec((tm, ky), lambda i, j: (i, 0)),
                  pl.BlockSpec((tm, ka), lambda i, j: (i, 0)),
                  pl.BlockSpec((ky, tn), lambda i, j: (0, j)),
                  pl.BlockSpec((ky, tn), lambda i, j: (0, nj + j)),
                  pl.BlockSpec((ka, tn), lambda i, j: (0, j)),
                  pl.BlockSpec((tm, tn), lambda i, j: (i, col_gs // tn + j)),
                  pl.BlockSpec((tm, tn), lambda i, j: (i, col_ga // tn + j))],
        out_specs=pl.BlockSpec((tm, tn), lambda i, j: (i, j)),
        out_shape=jax.ShapeDtypeStruct((m, d), BF16),
        compiler_params=_params(2),
        name="gated_merge",
    )(y, attn, w_glu, w_glu, w_ao, proj, proj)


def _ffn_up_kernel(h_ref, wa_ref, wg_ref, ca_ref, cg_ref, o_ref, ua_scr, ug_scr, halo_a, halo_g,
                   *, tm, blocks_per_seq):
    i = pl.program_id(0)
    j = pl.program_id(1)
    h = h_ref[...]
    seq_start = (i % blocks_per_seq) == 0

    def conv(w_ref, c_ref, u_scr, halo):
        u_scr[pl.ds(HALO_ROWS, tm), :] = jnp.dot(h, w_ref[...].astype(BF16), preferred_element_type=F32)
        u_scr[pl.ds(0, HALO_ROWS), :] = jnp.where(seq_start, 0.0, halo[j])
        halo[j] = u_scr[pl.ds(tm, HALO_ROWS), :]
        cw = c_ref[...]
        return (cw[2:3, :] * u_scr[pl.ds(HALO_ROWS, tm), :]
                + cw[1:2, :] * u_scr[pl.ds(HALO_ROWS - 1, tm), :]
                + cw[0:1, :] * u_scr[pl.ds(HALO_ROWS - 2, tm), :]
                + cw[3:4, :])

    a = conv(wa_ref, ca_ref, ua_scr, halo_a)
    gv = conv(wg_ref, cg_ref, ug_scr, halo_g)
    o_ref[...] = (a * _sigmoid(a) * gv).astype(o_ref.dtype)


def ffn_up(h, w_up, conv_p, seq, tm, tn):
    m, k = h.shape
    f = w_up.shape[1] // 2
    nj = -(-f // tn)
    assert f % LANES == 0 and f >= tn

    def col0(j, half=0):
        return pl.multiple_of(half * f + jnp.minimum(j * tn, f - tn), LANES)

    lane_tile = pl.Element(tn)
    u_scratch = pltpu.VMEM((tm + HALO_ROWS, tn), F32)
    halo_scratch = pltpu.VMEM((nj, HALO_ROWS, tn), F32)
    return pl.pallas_call(
        functools.partial(_ffn_up_kernel, tm=tm, blocks_per_seq=seq // tm),
        grid=(m // tm, nj),
        in_specs=[pl.BlockSpec((tm, k), lambda i, j: (i, 0)),
                  pl.BlockSpec((pl.Element(k), lane_tile), lambda i, j: (0, col0(j))),
                  pl.BlockSpec((pl.Element(k), lane_tile), lambda i, j: (0, col0(j, 1))),
                  pl.BlockSpec((pl.Element(HALO_ROWS), lane_tile), lambda i, j: (0, col0(j))),
                  pl.BlockSpec((pl.Element(HALO_ROWS), lane_tile), lambda i, j: (0, col0(j, 1)))],
        out_specs=pl.BlockSpec((pl.Element(tm), lane_tile), lambda i, j: (i * tm, col0(j))),
        out_shape=jax.ShapeDtypeStruct((m, f), BF16),
        scratch_shapes=[u_scratch, u_scratch, halo_scratch, halo_scratch],
        compiler_params=_params(2),
        name="ffn_up",
    )(h, w_up, w_up, conv_p, conv_p)


def kernel(x, g_mix, w_in, ssm_log_dt, ssm_a_re, ssm_a_im, ssm_b_re, ssm_b_im, ssm_c_re, ssm_c_im,
           ssm_d, w_glu, w_attn_out, w_out, g_ffn, w_up, conv_w, conv_b, w_down, g_final):
    b, l, d = x.shape
    depth = g_mix.shape[0]
    ssm_width = ssm_a_re.shape[1] * ssm_b_re.shape[-1]
    attn_width = w_attn_out.shape[1]
    n_heads = attn_width // HEAD_DIM
    qkv_width = len(DILATION_PATTERNS) * attn_width
    d_ff = w_down.shape[1]
    ff_tile = 512
    proj_tile = 1024
    ssm_tiles = ssm_width // proj_tile
    qkv_tiles = 3 * qkv_width // proj_tile
    gate_tiles = 2 * d // proj_tile

    xf = x.reshape(b * l, d).astype(F32)
    for i in range(depth):
        h = rmsnorm(xf, g_mix[i], BF16)
        w_in_b = w_in[i].astype(BF16)
        proj = matmul(h, w_in_b, BF16, 1024, proj_tile, ssm_tiles + gate_tiles,
                      lambda j: jnp.where(j < ssm_tiles, j, j + qkv_tiles), "in_proj")
        qkv = qkv_projection(h, w_in_b, ssm_tiles, qkv_width, l, 1024, proj_tile)
        mats = s5_prepare(ssm_log_dt[i], ssm_a_re[i], ssm_a_im[i], ssm_b_re[i], ssm_b_im[i],
                          ssm_c_re[i], ssm_c_im[i], ssm_d[i])
        y = s5_mixer_gelu(proj, 0, b, l, mats)
        attn = dilated_attention(qkv, b, l, n_heads)
        mixed = gated_merge(y, attn, w_glu[i].astype(BF16), w_attn_out[i].astype(BF16), proj,
                            ssm_width, ssm_width + d, 1024, 512)
        xf = matmul_residual(mixed, w_out[i].astype(BF16), xf, 1024, 512, 1, "out_proj")

        h = rmsnorm(xf, g_ffn[i], BF16)
        zrow = jnp.zeros((HALO_ROWS - CONV_WIDTH - 1, 2 * d_ff), F32)
        conv_p = jnp.concatenate([conv_w[i].astype(F32), conv_b[i][None].astype(F32), zrow], axis=0)
        act = ffn_up(h, w_up[i].astype(BF16), conv_p, l, 1024, ff_tile)
        xf = matmul_residual(act, w_down[i].astype(BF16), xf, 1024, 256, 1, "ffn_down", a_single_buffer=True)
    out = rmsnorm(xf, g_final, x.dtype)
    return out.reshape(b, l, d)
```

```python
import functools
import math

import jax
import jax.numpy as jnp
from jax import lax
from jax.experimental import pallas as pl
from jax.experimental.pallas import tpu as pltpu

F32 = jnp.float32
BF16 = jnp.bfloat16

RMS_EPS = 1e-5
LANES = 128
HEAD_DIM = 128
ROPE_DIM = HEAD_DIM // 4
ROPE_THETA = 500000.0
DILATION_PATTERNS = ((128, 1), (512, 4), (2048, 16))
ATTN_BLOCK = 128
ATTN_UNROLL = 8
ATTN_ORDER = tuple(sorted(range(len(DILATION_PATTERNS)), key=lambda i: -DILATION_PATTERNS[i][1]))
S5_CHUNK = 16
S5_SLAB_GROUPS = 8
CONV_WIDTH = 3
HALO_ROWS = 8
NEG_BIG = -1e30
VMEM_LIMIT = 56 * 1024 * 1024


def _params(n_axes):
    return pltpu.CompilerParams(dimension_semantics=("arbitrary",) * n_axes,
                                vmem_limit_bytes=VMEM_LIMIT)


def _rmsnorm_kernel(x_ref, g_ref, o_ref):
    x = x_ref[...]
    y = x * lax.rsqrt(jnp.mean(x * x, axis=-1, keepdims=True) + RMS_EPS)
    o_ref[...] = (y * g_ref[...]).astype(o_ref.dtype)


def rmsnorm(x2d, g, out_dtype, tm=256):
    m, d = x2d.shape
    return pl.pallas_call(
        _rmsnorm_kernel,
        grid=(m // tm,),
        in_specs=[pl.BlockSpec((tm, d), lambda i: (i, 0)),
                  pl.BlockSpec((1, d), lambda i: (0, 0))],
        out_specs=pl.BlockSpec((tm, d), lambda i: (i, 0)),
        out_shape=jax.ShapeDtypeStruct((m, d), out_dtype),
        compiler_params=_params(1),
        name="rmsnorm",
    )(x2d, g.reshape(1, d).astype(F32))


def _mm_kernel(a_ref, w_ref, o_ref):
    o_ref[...] = jnp.dot(a_ref[...], w_ref[...], preferred_element_type=F32).astype(o_ref.dtype)


def matmul(a, w, out_dtype, tm, tn, n_col_tiles, w_col_tile, name):
    m, k = a.shape
    return pl.pallas_call(
        _mm_kernel,
        grid=(m // tm, n_col_tiles),
        in_specs=[pl.BlockSpec((tm, k), lambda i, j: (i, 0)),
                  pl.BlockSpec((k, tn), lambda i, j: (0, w_col_tile(j)))],
        out_specs=pl.BlockSpec((tm, tn), lambda i, j: (i, j)),
        out_shape=jax.ShapeDtypeStruct((m, n_col_tiles * tn), out_dtype),
        compiler_params=_params(2),
        name=name,
    )(a, w)


def _mm_res_kernel(a_ref, w_ref, r_ref, o_ref):
    k = pl.program_id(2)
    part = jnp.dot(a_ref[...], w_ref[...], preferred_element_type=F32)

    @pl.when(k == 0)
    def _():
        o_ref[...] = r_ref[...] + part

    @pl.when(k != 0)
    def _():
        o_ref[...] += part


def matmul_residual(a, w, res, tm, tn, nk, name, a_single_buffer=False):
    m = a.shape[0]
    k, n = w.shape
    tk = k // nk
    a_mode = pl.Buffered(1) if a_single_buffer else None
    return pl.pallas_call(
        _mm_res_kernel,
        grid=(m // tm, n // tn, nk),
        in_specs=[pl.BlockSpec((tm, tk), lambda i, j, kk: (i, kk), pipeline_mode=a_mode),
                  pl.BlockSpec((tk, tn), lambda i, j, kk: (kk, j)),
                  pl.BlockSpec((tm, tn), lambda i, j, kk: (i, j))],
        out_specs=pl.BlockSpec((tm, tn), lambda i, j, kk: (i, j)),
        out_shape=jax.ShapeDtypeStruct((m, n), F32),
        compiler_params=_params(3),
        name=name,
    )(a, w, res)


def rope_tables(seq):
    half = ROPE_DIM // 2
    inv_freq = ROPE_THETA ** (-jnp.arange(0, ROPE_DIM, 2, dtype=F32) / ROPE_DIM)
    ang = jnp.arange(seq, dtype=F32)[:, None] * inv_freq[None, :]
    cos, sin = jnp.cos(ang), jnp.sin(ang)
    pad = HEAD_DIM - ROPE_DIM
    cos_t = jnp.concatenate([cos, cos, jnp.ones((seq, pad), F32)], axis=1)
    zeros_h = jnp.zeros((seq, half), F32)
    zeros_p = jnp.zeros((seq, pad), F32)
    sin_lo = jnp.concatenate([-sin, zeros_h, zeros_p], axis=1)
    sin_hi = jnp.concatenate([zeros_h, sin, zeros_p], axis=1)
    return cos_t, sin_lo, sin_hi


def _rotary(x, cos_t, sin_lo, sin_hi):
    half = ROPE_DIM // 2
    up = pltpu.roll(x, HEAD_DIM - half, axis=1)
    dn = pltpu.roll(x, half, axis=1)
    return x * cos_t + up * sin_lo + dn * sin_hi


def _qkv_proj_kernel(h_ref, w_ref, cos_ref, slo_ref, shi_ref, o_ref, *, heads_per_tile):
    acc = jnp.dot(h_ref[...], w_ref[...], preferred_element_type=F32)
    cos_t, sin_lo, sin_hi = cos_ref[...], slo_ref[...], shi_ref[...]
    for hh in range(heads_per_tile):
        sl = slice(hh * HEAD_DIM, (hh + 1) * HEAD_DIM)
        o_ref[:, sl] = _rotary(acc[:, sl], cos_t, sin_lo, sin_hi)


def qkv_projection(h, w, col_tile0, qkv_width, seq, tm, tn):
    m, k = h.shape
    n_q_tiles = qkv_width // tn
    cos_t, sin_lo, sin_hi = rope_tables(seq)
    scale = HEAD_DIM ** -0.5
    zeros = jnp.zeros_like(cos_t)
    tables = (jnp.stack([cos_t * scale, cos_t, jnp.ones_like(cos_t)]),
              jnp.stack([sin_lo * scale, sin_lo, zeros]),
              jnp.stack([sin_hi * scale, sin_hi, zeros]))

    def tab_index(i, j):
        kind = jnp.where(j < n_q_tiles, 0, jnp.where(j < 2 * n_q_tiles, 1, 2))
        return (kind, i % (seq // tm), 0)

    tab_spec = pl.BlockSpec((None, tm, HEAD_DIM), tab_index)
    return pl.pallas_call(
        functools.partial(_qkv_proj_kernel, heads_per_tile=tn // HEAD_DIM),
        grid=(m // tm, 3 * n_q_tiles),
        in_specs=[pl.BlockSpec((tm, k), lambda i, j: (i, 0)),
                  pl.BlockSpec((k, tn), lambda i, j: (0, col_tile0 + j)),
                  tab_spec, tab_spec, tab_spec],
        out_specs=pl.BlockSpec((tm, tn), lambda i, j: (i, j)),
        out_shape=jax.ShapeDtypeStruct((m, 3 * qkv_width), F32),
        compiler_params=_params(2),
        name="qkv_proj",
    )(h, w, *tables)


def _gelu_tanh(x):
    c = math.sqrt(2.0 / math.pi)
    return 0.5 * x * (1.0 + jnp.tanh(c * (x + 0.044715 * (x * x * x))))


def _split_bf16(x):
    hi = x.astype(BF16)
    return hi, (x - hi.astype(F32)).astype(BF16)


def _s5_kernel(u_ref, bbn_ref, cl_ref, dsk_ref, wc_ref, vc_ref, a_ref, o_ref,
               xf_scr, bc_scr, sin_scr, m_scr, w_scr, vt_scr, *, rows, n_seq):
    t_ = S5_CHUNK
    half = bc_scr.shape[1] // 2
    n_pair = half // LANES
    nt = (((1,), (1,)), ((), ()))

    @pl.when(pl.program_id(1) == 0)
    def _():
        b_hi, b_lo = _split_bf16(bbn_ref[...])
        c_hi, c_lo = _split_bf16(cl_ref[...])
        kt = (lax.dot_general(b_hi, c_hi, nt, preferred_element_type=F32)
              + lax.dot_general(b_hi, c_lo, nt, preferred_element_type=F32)
              + lax.dot_general(b_lo, c_hi, nt, preferred_element_type=F32))
        brow = lax.broadcasted_iota(jnp.int32, (LANES, LANES), 0)
        bcol = lax.broadcasted_iota(jnp.int32, (LANES, LANES), 1)
        chan = LANES // S5_SLAB_GROUPS
        same_group = (brow // chan) == (bcol // chan)
        zeros = jnp.zeros((LANES, LANES), BF16)
        for lag in range(t_):
            blk = jnp.where(same_group, kt[:, lag * LANES:(lag + 1) * LANES], 0.0)
            if lag == 0:
                blk = blk + jnp.where(brow == bcol, dsk_ref[...], 0.0)
            blk = blk.astype(BF16)
            for tin in range(t_ - lag):
                tout = tin + lag
                m_scr[tin * LANES:(tin + 1) * LANES, tout * LANES:(tout + 1) * LANES] = blk
        for tout in range(0, t_, 2):
            m_scr[(tout + 1) * LANES:(tout + 2) * LANES, tout * LANES:(tout + 1) * LANES] = zeros
        kdim = m_scr.shape[0]
        row_group = (lax.broadcasted_iota(jnp.int32, (kdim, LANES), 0) // (LANES // S5_SLAB_GROUPS)) % S5_SLAB_GROUPS
        lane_half = lax.broadcasted_iota(jnp.int32, (kdim, LANES), 1) // (LANES // 2)
        diff = row_group - lane_half
        for ri in range(2):
            wv = wc_ref[ri]
            vv = vc_ref[ri]
            for k in range(n_pair):
                sel = diff == 2 * k
                cols = slice(ri * half + k * LANES, ri * half + (k + 1) * LANES)
                w_scr[:, cols] = jnp.where(sel, wv, 0.0).astype(BF16)
                vt_scr[:, cols] = jnp.where(sel, vv, 0.0).astype(BF16)

    all_rows = rows * n_seq
    xf_scr[...] = u_ref[...].astype(F32)
    planes = [xf_scr[pl.ds(t, all_rows, stride=t_), :].astype(BF16) for t in range(t_)]
    u = jnp.concatenate(planes, axis=1)
    bc_scr[...] = jnp.dot(u, w_scr[...], preferred_element_type=F32)

    ar = a_ref[0:1, :]
    ai = a_ref[1:2, :]

    def tile_body(i, carry):
        new_carry = []
        for q in range(n_seq):
            sr, si = carry[q]
            base = pl.multiple_of(q * rows + i * 8, 8)
            tile = bc_scr[pl.ds(base, 8), :]
            rows_r, rows_i = [], []
            for r in range(8):
                rows_r.append(sr)
                rows_i.append(si)
                br = tile[r:r + 1, :half]
                bi = tile[r:r + 1, half:]
                sr, si = ar * sr - ai * si + br, ar * si + ai * sr + bi
            sin_scr[pl.ds(base, 8), :] = jnp.concatenate(
                [jnp.concatenate(rows_r, axis=0), jnp.concatenate(rows_i, axis=0)], axis=1)
            new_carry.append((sr, si))
        return tuple(new_carry)

    zero = jnp.zeros((1, half), F32)
    lax.fori_loop(0, rows // 8, tile_body, ((zero, zero),) * n_seq)

    s_in = sin_scr[...].astype(BF16)
    tile_w = 2 * LANES
    for jt in range(t_ // 2):
        k_hi = (2 * jt + 2) * LANES
        cols = slice(jt * tile_w, (jt + 1) * tile_w)
        y = (jnp.dot(u[:, :k_hi], m_scr[0:k_hi, cols], preferred_element_type=F32)
             + lax.dot_general(s_in, vt_scr[cols, :], nt, preferred_element_type=F32))
        y = _gelu_tanh(y)
        for tt in range(2):
            xf_scr[pl.ds(2 * jt + tt, all_rows, stride=t_), :] = y[:, tt * LANES:(tt + 1) * LANES]
    o_ref[...] = xf_scr[...].astype(o_ref.dtype)


def s5_prepare(log_dt, a_re, a_im, b_re, b_im, c_re, c_im, d_skip):
    g, p = a_re.shape
    c = b_re.shape[-1]
    t_ = S5_CHUNK
    gl = S5_SLAB_GROUPS
    ns = g // gl
    assert gl * c == LANES and 2 * p == LANES
    a_re, a_im = a_re.astype(F32), a_im.astype(F32)
    dt = jnp.exp(log_dt.astype(F32))[:, None]
    ks = jnp.arange(t_ + 1, dtype=F32)[:, None, None]
    mag = jnp.exp(ks * (dt * a_re)[None])
    ang = ks * (dt * a_im)[None]
    pr, pi = mag * jnp.cos(ang), mag * jnp.sin(ang)
    lb_re, lb_im = pr[1], pi[1]
    den = a_re * a_re + a_im * a_im
    f_re = ((lb_re - 1.0) * a_re + lb_im * a_im) / den
    f_im = (lb_im * a_re - (lb_re - 1.0) * a_im) / den
    br = jnp.swapaxes(b_re.astype(F32), 1, 2)
    bi = jnp.swapaxes(b_im.astype(F32), 1, 2)
    bb_re = f_re[:, None, :] * br - f_im[:, None, :] * bi
    bb_im = f_re[:, None, :] * bi + f_im[:, None, :] * br
    cr, ci = c_re.astype(F32), c_im.astype(F32)
    cp_re = cr[None] * pr[:, :, None, :] - ci[None] * pi[:, :, None, :]
    cp_im = cr[None] * pi[:, :, None, :] + ci[None] * pr[:, :, None, :]
    prr = pr[t_ - 1::-1][:t_]
    pir = pi[t_ - 1::-1][:t_]
    w_re = prr[:, :, None, :] * bb_re[None] - pir[:, :, None, :] * bb_im[None]
    w_im = prr[:, :, None, :] * bb_im[None] + pir[:, :, None, :] * bb_re[None]

    def rows_tgc(x):
        return x.reshape(t_, ns, gl * c, p).transpose(1, 0, 2, 3).reshape(ns, t_ * gl * c, p)

    def twice(x):
        return jnp.concatenate([x, x], axis=-1)

    bbn = jnp.concatenate([bb_re, -bb_im], axis=-1).reshape(ns, gl * c, 2 * p)
    cl = jnp.concatenate([rows_tgc(cp_re[:t_]), rows_tgc(cp_im[:t_])], axis=-1)
    dsk = d_skip.astype(F32).reshape(ns, 1, gl * c)
    wc = jnp.stack([twice(rows_tgc(w_re)), twice(rows_tgc(w_im))], axis=1)
    vc = jnp.stack([twice(rows_tgc(cp_re[1:])), twice(rows_tgc(-cp_im[1:]))], axis=1)
    a_mat = jnp.stack([pr[t_], pi[t_]], axis=0).reshape(2, ns, gl * p).transpose(1, 0, 2)
    return bbn, cl, dsk, wc, vc, a_mat


def s5_mixer_gelu(proj, col_block0, n_batch, seq, mats):
    bbn, cl, dsk, wc, vc, a_mat = mats
    ns = bbn.shape[0]
    kdim = wc.shape[2]
    sdim = 2 * a_mat.shape[2]
    rows = seq // S5_CHUNK
    n_seq = 2 if n_batch % 2 == 0 else 1
    return pl.pallas_call(
        functools.partial(_s5_kernel, rows=rows, n_seq=n_seq),
        grid=(ns, n_batch // n_seq),
        in_specs=[pl.BlockSpec((n_seq * seq, LANES), lambda s, b: (b, col_block0 + s)),
                  pl.BlockSpec((None, LANES, LANES), lambda s, b: (s, 0, 0)),
                  pl.BlockSpec((None, kdim, LANES), lambda s, b: (s, 0, 0)),
                  pl.BlockSpec((None, 1, LANES), lambda s, b: (s, 0, 0)),
                  pl.BlockSpec((None, 2, kdim, LANES), lambda s, b: (s, 0, 0, 0)),
                  pl.BlockSpec((None, 2, kdim, LANES), lambda s, b: (s, 0, 0, 0)),
                  pl.BlockSpec((None, 2, sdim // 2), lambda s, b: (s, 0, 0))],
        out_specs=pl.BlockSpec((n_seq * seq, LANES), lambda s, b: (b, s)),
        out_shape=jax.ShapeDtypeStruct((n_batch * seq, ns * LANES), BF16),
        scratch_shapes=[pltpu.VMEM((n_seq * seq, LANES), F32),
                        pltpu.VMEM((n_seq * rows, sdim), F32),
                        pltpu.VMEM((n_seq * rows, sdim), F32),
                        pltpu.VMEM((kdim, kdim), BF16),
                        pltpu.VMEM((kdim, sdim), BF16),
                        pltpu.VMEM((kdim, sdim), BF16)],
        compiler_params=_params(2),
        name="s5_mixer",
    )(proj, bbn, cl, dsk, wc, vc, a_mat)


def _attn_kernel(q_ref, k_ref, v_ref, o_ref, acc_scr, m_scr, l_scr, *, seq):
    step = pl.program_id(2)
    n_g = len(DILATION_PATTERNS)
    blk = ATTN_BLOCK
    row = lax.broadcasted_iota(jnp.int32, (blk, blk), 0)
    col = lax.broadcasted_iota(jnp.int32, (blk, blk), 1)
    cur_ok = col <= row
    prev_ok = col >= row
    nt = (((1,), (1,)), ((), ()))
    ones = jnp.ones((blk, HEAD_DIM), BF16)

    def run_group(step_i, dil):
        first = step_i == 0
        last = step_i == n_g - 1
        n_qb = seq // dil // blk

        def load(ref, start):
            if dil == 1:
                return ref[pl.ds(start, blk), :]
            return ref[pl.ds(start, blk, stride=dil), :]

        def store(ref, start, val):
            if dil == 1:
                ref[pl.ds(start, blk), :] = val
            else:
                ref[pl.ds(start, blk, stride=dil), :] = val

        def block(base, prev, prev_cond):
            q = load(q_ref, base).astype(BF16)
            kc = load(k_ref, base).astype(BF16)
            vc = jnp.concatenate([load(v_ref, base).astype(BF16), ones], axis=1)
            s_c = jnp.where(cur_ok, lax.dot_general(q, kc, nt, preferred_element_type=F32), NEG_BIG)
            m_blk = jnp.max(s_c, axis=1, keepdims=True)
            if prev is not None:
                ok = prev_ok if prev_cond is None else jnp.logical_and(prev_ok, prev_cond)
                s_p = jnp.where(ok, lax.dot_general(q, prev[0], nt, preferred_element_type=F32), NEG_BIG)
                m_blk = jnp.maximum(m_blk, jnp.max(s_p, axis=1, keepdims=True))
            if first:
                m_new = m_blk
            else:
                m_old = load(m_scr, base)
                m_new = jnp.maximum(m_old, m_blk)
            pv = jnp.dot(jnp.exp(s_c - m_new).astype(BF16), vc, preferred_element_type=F32)
            if prev is not None:
                pv = pv + jnp.dot(jnp.exp(s_p - m_new).astype(BF16), prev[1], preferred_element_type=F32)
            acc_new = pv[:, :HEAD_DIM]
            l_new = pv[:, HEAD_DIM:]
            if not first:
                alpha = jnp.exp(m_old - m_new)
                acc_new = alpha * load(acc_scr, base) + acc_new
                l_new = alpha * load(l_scr, base) + l_new
            if last and dil == 1:
                o_ref[pl.ds(pl.multiple_of(base, blk), blk), :] = (acc_new / l_new).astype(o_ref.dtype)
            else:
                store(acc_scr, base, acc_new)
                store(l_scr, base, l_new)
                if not last:
                    store(m_scr, base, jnp.broadcast_to(m_new, (blk, blk)))
            return kc, vc

        if n_qb >= ATTN_UNROLL:
            assert n_qb % ATTN_UNROLL == 0

            def res_body(res, _):
                def body(it, carry):
                    prev = carry
                    for u in range(ATTN_UNROLL):
                        base = (it * ATTN_UNROLL + u) * (blk * dil) + res
                        prev = block(base, prev, (it > 0) if u == 0 else None)
                    return prev
                init = (jnp.zeros((blk, HEAD_DIM), BF16), jnp.zeros((blk, 2 * HEAD_DIM), BF16))
                lax.fori_loop(0, n_qb // ATTN_UNROLL, body, init)
                return 0
            if dil == 1:
                res_body(0, 0)
            else:
                lax.fori_loop(0, dil, res_body, 0)
        else:
            assert ATTN_UNROLL % n_qb == 0
            res_per_body = ATTN_UNROLL // n_qb
            assert dil % res_per_body == 0

            def body(it, _):
                for rr in range(res_per_body):
                    res = it * res_per_body + rr
                    prev = None
                    for u in range(n_qb):
                        prev = block(u * (blk * dil) + res, prev, None)
                return 0
            lax.fori_loop(0, dil // res_per_body, body, 0)

    for step_i, gi in enumerate(ATTN_ORDER):
        window, dil = DILATION_PATTERNS[gi]
        assert window // dil == blk
        pl.when(step == step_i)(functools.partial(run_group, step_i, dil))

    if DILATION_PATTERNS[ATTN_ORDER[-1]][1] != 1:
        @pl.when(step == n_g - 1)
        def _():
            o_ref[...] = (acc_scr[...] / l_scr[...]).astype(o_ref.dtype)


def dilated_attention(qkv, n_batch, seq, n_heads):
    n_g = len(DILATION_PATTERNS)
    width = n_g * n_heads

    def group_of_step(s):
        gi = ATTN_ORDER[-1]
        for step_i in range(n_g - 2, -1, -1):
            gi = jnp.where(s == step_i, ATTN_ORDER[step_i], gi)
        return gi

    def col_spec(part):
        return pl.BlockSpec((seq, LANES), lambda b, h, s: (b, part * width + group_of_step(s) * n_heads + h))

    return pl.pallas_call(
        functools.partial(_attn_kernel, seq=seq),
        grid=(n_batch, n_heads, n_g),
        in_specs=[col_spec(0), col_spec(1), col_spec(2)],
        out_specs=pl.BlockSpec((seq, LANES), lambda b, h, g: (b, h)),
        out_shape=jax.ShapeDtypeStruct((n_batch * seq, n_heads * LANES), BF16),
        scratch_shapes=[pltpu.VMEM((seq, LANES), F32)] * 3,
        compiler_params=_params(3),
        name="dilated_attention",
    )(qkv, qkv, qkv)


def _sigmoid(x):
    return 1.0 / (1.0 + jnp.exp(-x))


def _mix_kernel(y_ref, at_ref, wa_ref, wb_ref, wo_ref, gs_ref, ga_ref, o_ref):
    y = y_ref[...]
    glu_a = jnp.dot(y, wa_ref[...], preferred_element_type=F32)
    glu_b = jnp.dot(y, wb_ref[...], preferred_element_type=F32)
    attn = jnp.dot(at_ref[...], wo_ref[...], preferred_element_type=F32)
    ssm = glu_a * _sigmoid(glu_b)
    mixed = _sigmoid(gs_ref[...].astype(F32)) * ssm + _sigmoid(ga_ref[...].astype(F32)) * attn
    o_ref[...] = mixed.astype(o_ref.dtype)


def gated_merge(y, attn, w_glu, w_ao, proj, col_gs, col_ga, tm, tn):
    m, ky = y.shape
    ka = attn.shape[1]
    d = w_ao.shape[1]
    nj = d // tn
    return pl.pallas_call(
        _mix_kernel,
        grid=(m // tm, nj),
        in_specs=[pl.BlockSpec((tm, ky), lambda i, j: (i, 0)),
                  pl.BlockSpec((tm, ka), lambda i, j: (i, 0)),
                  pl.BlockSpec((ky, tn), lambda i, j: (0, j)),
                  pl.BlockSpec((ky, tn), lambda i, j: (0, nj + j)),
                  pl.BlockSpec((ka, tn), lambda i, j: (0, j)),
                  pl.BlockSpec((tm, tn), lambda i, j: (i, col_gs // tn + j)),
                  pl.BlockSpec((tm, tn), lambda i, j: (i, col_ga // tn + j))],
        out_specs=pl.BlockSpec((tm, tn), lambda i, j: (i, j)),
        out_shape=jax.ShapeDtypeStruct((m, d), BF16),
        compiler_params=_params(2),
        name="gated_merge",
    )(y, attn, w_glu, w_glu, w_ao, proj, proj)


def _ffn_up_kernel(h_ref, wa_ref, wg_ref, ca_ref, cg_ref, o_ref, o_tail_ref, ua_scr, ug_scr, halo_a, halo_g,
                   *, tm, blocks_per_seq):
    i = pl.program_id(0)
    j = pl.program_id(1)
    last_j = pl.num_programs(1) - 1
    h = h_ref[...]
    seq_start = (i % blocks_per_seq) == 0

    def conv(w_ref, c_ref, u_scr, halo):
        u_scr[pl.ds(HALO_ROWS, tm), :] = jnp.dot(h, w_ref[...].astype(BF16), preferred_element_type=F32)
        u_scr[pl.ds(0, HALO_ROWS), :] = jnp.where(seq_start, 0.0, halo[j])
        halo[j] = u_scr[pl.ds(tm, HALO_ROWS), :]
        cw = c_ref[...]
        return (cw[2:3, :] * u_scr[pl.ds(HALO_ROWS, tm), :]
                + cw[1:2, :] * u_scr[pl.ds(HALO_ROWS - 1, tm), :]
                + cw[0:1, :] * u_scr[pl.ds(HALO_ROWS - 2, tm), :]
                + cw[3:4, :])

    a = conv(wa_ref, ca_ref, ua_scr, halo_a)
    gv = conv(wg_ref, cg_ref, ug_scr, halo_g)
    act = (a * _sigmoid(a) * gv).astype(o_ref.dtype)

    @pl.when(j != last_j)
    def _():
        o_ref[...] = act

    @pl.when(j == last_j)
    def _():
        o_tail_ref[...] = act


def ffn_up(h, w_up, conv_p, seq, tm, tn):
    m, k = h.shape
    f = w_up.shape[1] // 2
    nj = -(-f // tn)
    assert f % LANES == 0 and nj >= 2

    def col0(j, half=0):
        return pl.multiple_of(half * f + jnp.minimum(j * tn, f - tn), LANES)

    lane_tile = pl.Element(tn)
    u_scratch = pltpu.VMEM((tm + HALO_ROWS, tn), F32)
    halo_scratch = pltpu.VMEM((nj, HALO_ROWS, tn), F32)
    return pl.pallas_call(
        functools.partial(_ffn_up_kernel, tm=tm, blocks_per_seq=seq // tm),
        grid=(m // tm, nj),
        in_specs=[pl.BlockSpec((tm, k), lambda i, j: (i, 0)),
                  pl.BlockSpec((pl.Element(k), lane_tile), lambda i, j: (0, col0(j))),
                  pl.BlockSpec((pl.Element(k), lane_tile), lambda i, j: (0, col0(j, 1))),
                  pl.BlockSpec((pl.Element(HALO_ROWS), lane_tile), lambda i, j: (0, col0(j))),
                  pl.BlockSpec((pl.Element(HALO_ROWS), lane_tile), lambda i, j: (0, col0(j, 1)))],
        out_specs=[pl.BlockSpec((tm, tn), lambda i, j: (i, jnp.minimum(j, nj - 2))),
                   pl.BlockSpec((tm, tn), lambda i, j: (i, 0))],
        out_shape=[jax.ShapeDtypeStruct((m, (nj - 1) * tn), BF16),
                   jax.ShapeDtypeStruct((m, tn), BF16)],
        scratch_shapes=[u_scratch, u_scratch, halo_scratch, halo_scratch],
        compiler_params=_params(2),
        name="ffn_up",
    )(h, w_up, w_up, conv_p, conv_p)


def _ffn_down_kernel(a1_ref, a2_ref, w1_ref, w2_ref, r_ref, o_ref):
    o_ref[...] = (r_ref[...]
                  + jnp.dot(a1_ref[...], w1_ref[...], preferred_element_type=F32)
                  + jnp.dot(a2_ref[...], w2_ref[...], preferred_element_type=F32))


def ffn_down(a_main, a_tail, w, res, tm, tn):
    m, k1 = a_main.shape
    k, n = w.shape
    k2 = k - k1
    t_tail = a_tail.shape[1]
    assert 0 < k2 <= t_tail and (t_tail - k2) % k2 == 0 and k1 % k2 == 0
    return pl.pallas_call(
        _ffn_down_kernel,
        grid=(m // tm, n // tn),
        in_specs=[pl.BlockSpec((tm, k1), lambda i, j: (i, 0), pipeline_mode=pl.Buffered(1)),
                  pl.BlockSpec((tm, k2), lambda i, j: (i, (t_tail - k2) // k2)),
                  pl.BlockSpec((k1, tn), lambda i, j: (0, j)),
                  pl.BlockSpec((k2, tn), lambda i, j: (k1 // k2, j)),
                  pl.BlockSpec((tm, tn), lambda i, j: (i, j))],
        out_specs=pl.BlockSpec((tm, tn), lambda i, j: (i, j)),
        out_shape=jax.ShapeDtypeStruct((m, n), F32),
        compiler_params=_params(2),
        name="ffn_down",
    )(a_main, a_tail, w, w, res)


def kernel(x, g_mix, w_in, ssm_log_dt, ssm_a_re, ssm_a_im, ssm_b_re, ssm_b_im, ssm_c_re, ssm_c_im,
           ssm_d, w_glu, w_attn_out, w_out, g_ffn, w_up, conv_w, conv_b, w_down, g_final):
    b, l, d = x.shape
    depth = g_mix.shape[0]
    ssm_width = ssm_a_re.shape[1] * ssm_b_re.shape[-1]
    attn_width = w_attn_out.shape[1]
    n_heads = attn_width // HEAD_DIM
    qkv_width = len(DILATION_PATTERNS) * attn_width
    d_ff = w_down.shape[1]
    ff_tile = 512
    proj_tile = 1024
    ssm_tiles = ssm_width // proj_tile
    qkv_tiles = 3 * qkv_width // proj_tile
    gate_tiles = 2 * d // proj_tile

    xf = x.reshape(b * l, d).astype(F32)
    for i in range(depth):
        h = rmsnorm(xf, g_mix[i], BF16)
        w_in_b = w_in[i].astype(BF16)
        proj = matmul(h, w_in_b, BF16, 1024, proj_tile, ssm_tiles + gate_tiles,
                      lambda j: jnp.where(j < ssm_tiles, j, j + qkv_tiles), "in_proj")
        qkv = qkv_projection(h, w_in_b, ssm_tiles, qkv_width, l, 1024, proj_tile)
        mats = s5_prepare(ssm_log_dt[i], ssm_a_re[i], ssm_a_im[i], ssm_b_re[i], ssm_b_im[i],
                          ssm_c_re[i], ssm_c_im[i], ssm_d[i])
        y = s5_mixer_gelu(proj, 0, b, l, mats)
        attn = dilated_attention(qkv, b, l, n_heads)
        mixed = gated_merge(y, attn, w_glu[i].astype(BF16), w_attn_out[i].astype(BF16), proj,
                            ssm_width, ssm_width + d, 1024, 512)
        xf = matmul_residual(mixed, w_out[i].astype(BF16), xf, 1024, 512, 1, "out_proj")

        h = rmsnorm(xf, g_ffn[i], BF16)
        zrow = jnp.zeros((HALO_ROWS - CONV_WIDTH - 1, 2 * d_ff), F32)
        conv_p = jnp.concatenate([conv_w[i].astype(F32), conv_b[i][None].astype(F32), zrow], axis=0)
        act_main, act_tail = ffn_up(h, w_up[i].astype(BF16), conv_p, l, 1024, ff_tile)
        xf = ffn_down(act_main, act_tail, w_down[i].astype(BF16), xf, 1024, 256)
    out = rmsnorm(xf, g_final, x.dtype)
    return out.reshape(b, l, d)
```

```python
import functools
import math

import jax
import jax.numpy as jnp
from jax import lax
from jax.experimental import pallas as pl
from jax.experimental.pallas import tpu as pltpu

F32 = jnp.float32
BF16 = jnp.bfloat16

RMS_EPS = 1e-5
LANES = 128
HEAD_DIM = 128
ROPE_DIM = HEAD_DIM // 4
ROPE_THETA = 500000.0
DILATION_PATTERNS = ((128, 1), (512, 4), (2048, 16))
ATTN_BLOCK = 128
ATTN_UNROLL = 8
ATTN_ORDER = tuple(sorted(range(len(DILATION_PATTERNS)), key=lambda i: -DILATION_PATTERNS[i][1]))
S5_CHUNK = 16
S5_SLAB_GROUPS = 8
CONV_WIDTH = 3
HALO_ROWS = 8
NEG_BIG = -1e30
VMEM_LIMIT = 56 * 1024 * 1024


def _params(n_axes):
    return pltpu.CompilerParams(dimension_semantics=("arbitrary",) * n_axes,
                                vmem_limit_bytes=VMEM_LIMIT)


def _rmsnorm_kernel(x_ref, g_ref, o_ref):
    x = x_ref[...]
    y = x * lax.rsqrt(jnp.mean(x * x, axis=-1, keepdims=True) + RMS_EPS)
    o_ref[...] = (y * g_ref[...]).astype(o_ref.dtype)


def rmsnorm(x2d, g, out_dtype, tm=512):
    m, d = x2d.shape
    return pl.pallas_call(
        _rmsnorm_kernel,
        grid=(m // tm,),
        in_specs=[pl.BlockSpec((tm, d), lambda i: (i, 0)),
                  pl.BlockSpec((1, d), lambda i: (0, 0))],
        out_specs=pl.BlockSpec((tm, d), lambda i: (i, 0)),
        out_shape=jax.ShapeDtypeStruct((m, d), out_dtype),
        compiler_params=_params(1),
        name="rmsnorm",
    )(x2d, g.reshape(1, d).astype(F32))


def _with_side_casts(body, n_in, n_out, n_side):
    def kernel(*refs):
        ins = refs[:n_in]
        srcs = refs[n_in:n_in + n_side]
        outs = refs[n_in + n_side:n_in + n_side + n_out]
        dsts = refs[n_in + n_side + n_out:n_in + 2 * n_side + n_out]
        rest = refs[n_in + 2 * n_side + n_out:]
        body(*ins, *outs, *rest)
        for src, dst in zip(srcs, dsts):
            dst[...] = src[...].astype(dst.dtype)
    return kernel


def _cast_chunks(w, n_steps):
    bf16_rows = 16
    rows = next(r for r in range(bf16_rows, w.shape[0] + 1, bf16_rows)
                if w.shape[0] % r == 0 and w.shape[0] // r <= n_steps)
    return w, rows


def _side_cast_specs(side, grid):
    n_steps = math.prod(grid)

    def linear_step(*idx):
        s = idx[0]
        for extent, i in zip(grid[1:], idx[1:]):
            s = s * extent + i
        return s

    specs, shapes = [], []
    for src, rows in side:
        n_chunks = src.shape[0] // rows
        assert n_chunks * rows == src.shape[0] and n_chunks <= n_steps
        specs.append(pl.BlockSpec((rows, src.shape[1]),
                                  lambda *idx, n_chunks=n_chunks: (jnp.minimum(linear_step(*idx), n_chunks - 1), 0)))
        shapes.append(jax.ShapeDtypeStruct(src.shape, BF16))
    return specs, shapes


def _mm_kernel(a_ref, w_ref, o_ref):
    o_ref[...] = jnp.dot(a_ref[...], w_ref[...], preferred_element_type=F32).astype(o_ref.dtype)


def matmul(a, w, out_dtype, tm, tn, n_col_tiles, w_col_tile, name, side=()):
    m, k = a.shape
    grid = (m // tm, n_col_tiles)
    side_specs, side_shapes = _side_cast_specs(side, grid)
    return pl.pallas_call(
        _with_side_casts(_mm_kernel, 2, 1, len(side)),
        grid=grid,
        in_specs=[pl.BlockSpec((tm, k), lambda i, j: (i, 0)),
                  pl.BlockSpec((k, tn), lambda i, j: (0, w_col_tile(j)))] + side_specs,
        out_specs=[pl.BlockSpec((tm, tn), lambda i, j: (i, j))] + side_specs,
        out_shape=[jax.ShapeDtypeStruct((m, n_col_tiles * tn), out_dtype)] + side_shapes,
        compiler_params=_params(2),
        name=name,
    )(a, w, *[src for src, _ in side])


def _mm_res_kernel(a_ref, w_ref, r_ref, o_ref):
    k = pl.program_id(2)
    part = jnp.dot(a_ref[...], w_ref[...], preferred_element_type=F32)

    @pl.when(k == 0)
    def _():
        o_ref[...] = r_ref[...] + part

    @pl.when(k != 0)
    def _():
        o_ref[...] += part


def matmul_residual(a, w, res, tm, tn, nk, name, a_single_buffer=False):
    m = a.shape[0]
    k, n = w.shape
    tk = k // nk
    a_mode = pl.Buffered(1) if a_single_buffer else None
    return pl.pallas_call(
        _mm_res_kernel,
        grid=(m // tm, n // tn, nk),
        in_specs=[pl.BlockSpec((tm, tk), lambda i, j, kk: (i, kk), pipeline_mode=a_mode),
                  pl.BlockSpec((tk, tn), lambda i, j, kk: (kk, j)),
                  pl.BlockSpec((tm, tn), lambda i, j, kk: (i, j))],
        out_specs=pl.BlockSpec((tm, tn), lambda i, j, kk: (i, j)),
        out_shape=jax.ShapeDtypeStruct((m, n), F32),
        compiler_params=_params(3),
        name=name,
    )(a, w, res)


def rope_tables(seq):
    half = ROPE_DIM // 2
    inv_freq = ROPE_THETA ** (-jnp.arange(0, ROPE_DIM, 2, dtype=F32) / ROPE_DIM)
    ang = jnp.arange(seq, dtype=F32)[:, None] * inv_freq[None, :]
    cos, sin = jnp.cos(ang), jnp.sin(ang)
    pad = HEAD_DIM - ROPE_DIM
    cos_t = jnp.concatenate([cos, cos, jnp.ones((seq, pad), F32)], axis=1)
    zeros_h = jnp.zeros((seq, half), F32)
    zeros_p = jnp.zeros((seq, pad), F32)
    sin_lo = jnp.concatenate([-sin, zeros_h, zeros_p], axis=1)
    sin_hi = jnp.concatenate([zeros_h, sin, zeros_p], axis=1)
    return cos_t, sin_lo, sin_hi


def _rotary(x, cos_t, sin_lo, sin_hi):
    half = ROPE_DIM // 2
    up = pltpu.roll(x, HEAD_DIM - half, axis=1)
    dn = pltpu.roll(x, half, axis=1)
    return x * cos_t + up * sin_lo + dn * sin_hi


def _qkv_proj_kernel(h_ref, w_ref, cos_ref, slo_ref, shi_ref, o_ref, *, heads_per_tile):
    acc = jnp.dot(h_ref[...], w_ref[...], preferred_element_type=F32)
    cos_t, sin_lo, sin_hi = cos_ref[...], slo_ref[...], shi_ref[...]
    for hh in range(heads_per_tile):
        sl = slice(hh * HEAD_DIM, (hh + 1) * HEAD_DIM)
        o_ref[:, sl] = _rotary(acc[:, sl], cos_t, sin_lo, sin_hi)


def qkv_projection(h, w, col_tile0, qkv_width, seq, tm, tn, side=()):
    m, k = h.shape
    n_q_tiles = qkv_width // tn
    cos_t, sin_lo, sin_hi = rope_tables(seq)
    scale = HEAD_DIM ** -0.5
    zeros = jnp.zeros_like(cos_t)
    tables = (jnp.stack([cos_t * scale, cos_t, jnp.ones_like(cos_t)]),
              jnp.stack([sin_lo * scale, sin_lo, zeros]),
              jnp.stack([sin_hi * scale, sin_hi, zeros]))

    def tab_index(i, j):
        kind = jnp.where(j < n_q_tiles, 0, jnp.where(j < 2 * n_q_tiles, 1, 2))
        return (kind, i % (seq // tm), 0)

    tab_spec = pl.BlockSpec((None, tm, HEAD_DIM), tab_index)
    grid = (m // tm, 3 * n_q_tiles)
    side_specs, side_shapes = _side_cast_specs(side, grid)
    body = functools.partial(_qkv_proj_kernel, heads_per_tile=tn // HEAD_DIM)
    return pl.pallas_call(
        _with_side_casts(body, 5, 1, len(side)),
        grid=grid,
        in_specs=[pl.BlockSpec((tm, k), lambda i, j: (i, 0)),
                  pl.BlockSpec((k, tn), lambda i, j: (0, col_tile0 + j)),
                  tab_spec, tab_spec, tab_spec] + side_specs,
        out_specs=[pl.BlockSpec((tm, tn), lambda i, j: (i, j))] + side_specs,
        out_shape=[jax.ShapeDtypeStruct((m, 3 * qkv_width), F32)] + side_shapes,
        compiler_params=_params(2),
        name="qkv_proj",
    )(h, w, *tables, *[src for src, _ in side])


def _gelu_tanh(x):
    c = math.sqrt(2.0 / math.pi)
    return 0.5 * x * (1.0 + jnp.tanh(c * (x + 0.044715 * (x * x * x))))


def _split_bf16(x):
    hi = x.astype(BF16)
    return hi, (x - hi.astype(F32)).astype(BF16)


def _s5_kernel(u_ref, bbn_ref, cl_ref, dsk_ref, wc_ref, vc_ref, a_ref, o_ref,
               xf_scr, bc_scr, sin_scr, m_scr, w_scr, vt_scr, *, rows, n_seq):
    t_ = S5_CHUNK
    half = bc_scr.shape[1] // 2
    n_pair = half // LANES
    nt = (((1,), (1,)), ((), ()))

    @pl.when(pl.program_id(1) == 0)
    def _():
        b_hi, b_lo = _split_bf16(bbn_ref[...])
        c_hi, c_lo = _split_bf16(cl_ref[...])
        kt = (lax.dot_general(b_hi, c_hi, nt, preferred_element_type=F32)
              + lax.dot_general(b_hi, c_lo, nt, preferred_element_type=F32)
              + lax.dot_general(b_lo, c_hi, nt, preferred_element_type=F32))
        brow = lax.broadcasted_iota(jnp.int32, (LANES, LANES), 0)
        bcol = lax.broadcasted_iota(jnp.int32, (LANES, LANES), 1)
        chan = LANES // S5_SLAB_GROUPS
        same_group = (brow // chan) == (bcol // chan)
        zeros = jnp.zeros((LANES, LANES), BF16)
        for lag in range(t_):
            blk = jnp.where(same_group, kt[:, lag * LANES:(lag + 1) * LANES], 0.0)
            if lag == 0:
                blk = blk + jnp.where(brow == bcol, dsk_ref[...], 0.0)
            blk = blk.astype(BF16)
            for tin in range(t_ - lag):
                tout = tin + lag
                m_scr[tin * LANES:(tin + 1) * LANES, tout * LANES:(tout + 1) * LANES] = blk
        for tout in range(0, t_, 2):
            m_scr[(tout + 1) * LANES:(tout + 2) * LANES, tout * LANES:(tout + 1) * LANES] = zeros
        kdim = m_scr.shape[0]
        row_group = (lax.broadcasted_iota(jnp.int32, (kdim, LANES), 0) // (LANES // S5_SLAB_GROUPS)) % S5_SLAB_GROUPS
        lane_half = lax.broadcasted_iota(jnp.int32, (kdim, LANES), 1) // (LANES // 2)
        diff = row_group - lane_half
        for ri in range(2):
            wv = wc_ref[ri]
            vv = vc_ref[ri]
            for k in range(n_pair):
                sel = diff == 2 * k
                cols = slice(ri * half + k * LANES, ri * half + (k + 1) * LANES)
                w_scr[:, cols] = jnp.where(sel, wv, 0.0).astype(BF16)
                vt_scr[:, cols] = jnp.where(sel, vv, 0.0).astype(BF16)

    all_rows = rows * n_seq
    xf_scr[...] = u_ref[...].astype(F32)
    planes = [xf_scr[pl.ds(t, all_rows, stride=t_), :].astype(BF16) for t in range(t_)]
    u = jnp.concatenate(planes, axis=1)
    bc_scr[...] = jnp.dot(u, w_scr[...], preferred_element_type=F32)

    ar = a_ref[0:1, :]
    ai = a_ref[1:2, :]

    def tile_body(i, carry):
        new_carry = []
        for q in range(n_seq):
            sr, si = carry[q]
            base = pl.multiple_of(q * rows + i * 8, 8)
            tile = bc_scr[pl.ds(base, 8), :]
            rows_r, rows_i = [], []
            for r in range(8):
                rows_r.append(sr)
                rows_i.append(si)
                br = tile[r:r + 1, :half]
                bi = tile[r:r + 1, half:]
                sr, si = ar * sr - ai * si + br, ar * si + ai * sr + bi
            sin_scr[pl.ds(base, 8), :] = jnp.concatenate(
                [jnp.concatenate(rows_r, axis=0), jnp.concatenate(rows_i, axis=0)], axis=1)
            new_carry.append((sr, si))
        return tuple(new_carry)

    zero = jnp.zeros((1, half), F32)
    lax.fori_loop(0, rows // 8, tile_body, ((zero, zero),) * n_seq)

    s_in = sin_scr[...].astype(BF16)
    tile_w = 2 * LANES
    for jt in range(t_ // 2):
        k_hi = (2 * jt + 2) * LANES
        cols = slice(jt * tile_w, (jt + 1) * tile_w)
        y = (jnp.dot(u[:, :k_hi], m_scr[0:k_hi, cols], preferred_element_type=F32)
             + lax.dot_general(s_in, vt_scr[cols, :], nt, preferred_element_type=F32))
        y = _gelu_tanh(y)
        for tt in range(2):
            xf_scr[pl.ds(2 * jt + tt, all_rows, stride=t_), :] = y[:, tt * LANES:(tt + 1) * LANES]
    o_ref[...] = xf_scr[...].astype(o_ref.dtype)


def s5_prepare(log_dt, a_re, a_im, b_re, b_im, c_re, c_im, d_skip):
    g, p = a_re.shape
    c = b_re.shape[-1]
    t_ = S5_CHUNK
    gl = S5_SLAB_GROUPS
    ns = g // gl
    assert gl * c == LANES and 2 * p == LANES
    def twice(x):
        x = x.astype(F32)
        return jnp.concatenate([x, x], axis=-1)

    a_re, a_im = twice(a_re), twice(a_im)
    dt = jnp.exp(log_dt.astype(F32))[:, None]
    ks = jnp.arange(t_ + 1, dtype=F32)[:, None, None]
    mag = jnp.exp(ks * (dt * a_re)[None])
    ang = ks * (dt * a_im)[None]
    pr, pi = mag * jnp.cos(ang), mag * jnp.sin(ang)
    lb_re, lb_im = pr[1], pi[1]
    den = a_re * a_re + a_im * a_im
    f_re = ((lb_re - 1.0) * a_re + lb_im * a_im) / den
    f_im = (lb_im * a_re - (lb_re - 1.0) * a_im) / den
    br = twice(jnp.swapaxes(b_re, 1, 2))
    bi = twice(jnp.swapaxes(b_im, 1, 2))
    bb_re = f_re[:, None, :] * br - f_im[:, None, :] * bi
    bb_im = f_re[:, None, :] * bi + f_im[:, None, :] * br
    cr, ci = twice(c_re), twice(c_im)
    cp_re = cr[None] * pr[:, :, None, :] - ci[None] * pi[:, :, None, :]
    cp_im = cr[None] * pi[:, :, None, :] + ci[None] * pr[:, :, None, :]
    prr = pr[t_ - 1::-1][:t_]
    pir = pi[t_ - 1::-1][:t_]
    w_re = prr[:, :, None, :] * bb_re[None] - pir[:, :, None, :] * bb_im[None]
    w_im = prr[:, :, None, :] * bb_im[None] + pir[:, :, None, :] * bb_re[None]

    def rows_tgc(x):
        return x.reshape(t_, ns, gl * c, 2 * p).transpose(1, 0, 2, 3).reshape(ns, t_ * gl * c, 2 * p)

    first_copy = jnp.arange(2 * p) < p
    bbn = jnp.where(first_copy, bb_re, -bb_im).reshape(ns, gl * c, 2 * p)
    cl = rows_tgc(jnp.where(first_copy, cp_re[:t_], cp_im[:t_]))
    dsk = d_skip.astype(F32).reshape(ns, 1, gl * c)
    wc = jnp.stack([rows_tgc(w_re), rows_tgc(w_im)], axis=1)
    vc = jnp.stack([rows_tgc(cp_re[1:]), rows_tgc(-cp_im[1:])], axis=1)
    a_mat = jnp.stack([pr[t_][:, :p], pi[t_][:, :p]], axis=0).reshape(2, ns, gl * p).transpose(1, 0, 2)
    return bbn, cl, dsk, wc, vc, a_mat


def s5_mixer_gelu(proj, col_block0, n_batch, seq, mats):
    bbn, cl, dsk, wc, vc, a_mat = mats
    ns = bbn.shape[0]
    kdim = wc.shape[2]
    sdim = 2 * a_mat.shape[2]
    rows = seq // S5_CHUNK
    n_seq = 2 if n_batch % 2 == 0 else 1
    return pl.pallas_call(
        functools.partial(_s5_kernel, rows=rows, n_seq=n_seq),
        grid=(ns, n_batch // n_seq),
        in_specs=[pl.BlockSpec((n_seq * seq, LANES), lambda s, b: (b, col_block0 + s)),
                  pl.BlockSpec((None, LANES, LANES), lambda s, b: (s, 0, 0)),
                  pl.BlockSpec((None, kdim, LANES), lambda s, b: (s, 0, 0)),
                  pl.BlockSpec((None, 1, LANES), lambda s, b: (s, 0, 0)),
                  pl.BlockSpec((None, 2, kdim, LANES), lambda s, b: (s, 0, 0, 0)),
                  pl.BlockSpec((None, 2, kdim, LANES), lambda s, b: (s, 0, 0, 0)),
                  pl.BlockSpec((None, 2, sdim // 2), lambda s, b: (s, 0, 0))],
        out_specs=pl.BlockSpec((n_seq * seq, LANES), lambda s, b: (b, s)),
        out_shape=jax.ShapeDtypeStruct((n_batch * seq, ns * LANES), BF16),
        scratch_shapes=[pltpu.VMEM((n_seq * seq, LANES), F32),
                        pltpu.VMEM((n_seq * rows, sdim), F32),
                        pltpu.VMEM((n_seq * rows, sdim), F32),
                        pltpu.VMEM((kdim, kdim), BF16),
                        pltpu.VMEM((kdim, sdim), BF16),
                        pltpu.VMEM((kdim, sdim), BF16)],
        compiler_params=_params(2),
        name="s5_mixer",
    )(proj, bbn, cl, dsk, wc, vc, a_mat)


def _attn_kernel(q_ref, k_ref, v_ref, o_ref, acc_scr, m_scr, l_scr, *, seq):
    step = pl.program_id(2)
    n_g = len(DILATION_PATTERNS)
    blk = ATTN_BLOCK
    row = lax.broadcasted_iota(jnp.int32, (blk, blk), 0)
    col = lax.broadcasted_iota(jnp.int32, (blk, blk), 1)
    cur_ok = col <= row
    prev_ok = col >= row
    nt = (((1,), (1,)), ((), ()))
    ones = jnp.ones((blk, HEAD_DIM), BF16)

    def run_group(step_i, dil):
        first = step_i == 0
        last = step_i == n_g - 1
        n_qb = seq // dil // blk

        def load(ref, start):
            if dil == 1:
                return ref[pl.ds(start, blk), :]
            return ref[pl.ds(start, blk, stride=dil), :]

        def store(ref, start, val):
            if dil == 1:
                ref[pl.ds(start, blk), :] = val
            else:
                ref[pl.ds(start, blk, stride=dil), :] = val

        def block(base, prev, prev_cond):
            q = load(q_ref, base).astype(BF16)
            kc = load(k_ref, base).astype(BF16)
            vc = jnp.concatenate([load(v_ref, base).astype(BF16), ones], axis=1)
            s_c = jnp.where(cur_ok, lax.dot_general(q, kc, nt, preferred_element_type=F32), NEG_BIG)
            m_blk = jnp.max(s_c, axis=1, keepdims=True)
            if prev is not None:
                ok = prev_ok if prev_cond is None else jnp.logical_and(prev_ok, prev_cond)
                s_p = jnp.where(ok, lax.dot_general(q, prev[0], nt, preferred_element_type=F32), NEG_BIG)
                m_blk = jnp.maximum(m_blk, jnp.max(s_p, axis=1, keepdims=True))
            if first:
                m_new = m_blk
            else:
                m_old = load(m_scr, base)
                m_new = jnp.maximum(m_old, m_blk)
            pv = jnp.dot(jnp.exp(s_c - m_new).astype(BF16), vc, preferred_element_type=F32)
            if prev is not None:
                pv = pv + jnp.dot(jnp.exp(s_p - m_new).astype(BF16), prev[1], preferred_element_type=F32)
            acc_new = pv[:, :HEAD_DIM]
            l_new = pv[:, HEAD_DIM:]
            if not first:
                alpha = jnp.exp(m_old - m_new)
                acc_new = alpha * load(acc_scr, base) + acc_new
                l_new = alpha * load(l_scr, base) + l_new
            if last and dil == 1:
                o_ref[pl.ds(pl.multiple_of(base, blk), blk), :] = (acc_new / l_new).astype(o_ref.dtype)
            else:
                store(acc_scr, base, acc_new)
                store(l_scr, base, l_new)
                if not last:
                    store(m_scr, base, jnp.broadcast_to(m_new, (blk, blk)))
            return kc, vc

        if n_qb >= ATTN_UNROLL:
            assert n_qb % ATTN_UNROLL == 0

            def res_body(res, _):
                def body(it, carry):
                    prev = carry
                    for u in range(ATTN_UNROLL):
                        base = (it * ATTN_UNROLL + u) * (blk * dil) + res
                        prev = block(base, prev, (it > 0) if u == 0 else None)
                    return prev
                init = (jnp.zeros((blk, HEAD_DIM), BF16), jnp.zeros((blk, 2 * HEAD_DIM), BF16))
                lax.fori_loop(0, n_qb // ATTN_UNROLL, body, init)
                return 0
            if dil == 1:
                res_body(0, 0)
            else:
                lax.fori_loop(0, dil, res_body, 0)
        else:
            assert ATTN_UNROLL % n_qb == 0
            res_per_body = ATTN_UNROLL // n_qb
            assert dil % res_per_body == 0

            def body(it, _):
                for rr in range(res_per_body):
                    res = it * res_per_body + rr
                    prev = None
                    for u in range(n_qb):
                        prev = block(u * (blk * dil) + res, prev, None)
                return 0
            lax.fori_loop(0, dil // res_per_body, body, 0)

    for step_i, gi in enumerate(ATTN_ORDER):
        window, dil = DILATION_PATTERNS[gi]
        assert window // dil == blk
        pl.when(step == step_i)(functools.partial(run_group, step_i, dil))

    if DILATION_PATTERNS[ATTN_ORDER[-1]][1] != 1:
        @pl.when(step == n_g - 1)
        def _():
            o_ref[...] = (acc_scr[...] / l_scr[...]).astype(o_ref.dtype)


def dilated_attention(qkv, n_batch, seq, n_heads, side=()):
    n_g = len(DILATION_PATTERNS)
    width = n_g * n_heads

    def group_of_step(s):
        gi = ATTN_ORDER[-1]
        for step_i in range(n_g - 2, -1, -1):
            gi = jnp.where(s == step_i, ATTN_ORDER[step_i], gi)
        return gi

    def col_spec(part):
        return pl.BlockSpec((seq, LANES), lambda b, h, s: (b, part * width + group_of_step(s) * n_heads + h))

    grid = (n_batch, n_heads, n_g)
    side_specs, side_shapes = _side_cast_specs(side, grid)
    return pl.pallas_call(
        _with_side_casts(functools.partial(_attn_kernel, seq=seq), 3, 1, len(side)),
        grid=grid,
        in_specs=[col_spec(0), col_spec(1), col_spec(2)] + side_specs,
        out_specs=[pl.BlockSpec((seq, LANES), lambda b, h, g: (b, h))] + side_specs,
        out_shape=[jax.ShapeDtypeStruct((n_batch * seq, n_heads * LANES), BF16)] + side_shapes,
        scratch_shapes=[pltpu.VMEM((seq, LANES), F32)] * 3,
        compiler_params=_params(3),
        name="dilated_attention",
    )(qkv, qkv, qkv, *[src for src, _ in side])


def _sigmoid(x):
    return 1.0 / (1.0 + jnp.exp(-x))


def _mix_kernel(y_ref, at_ref, wa_ref, wb_ref, wo_ref, gs_ref, ga_ref, o_ref):
    y = y_ref[...]
    glu_a = jnp.dot(y, wa_ref[...], preferred_element_type=F32)
    glu_b = jnp.dot(y, wb_ref[...], preferred_element_type=F32)
    attn = jnp.dot(at_ref[...], wo_ref[...], preferred_element_type=F32)
    ssm = glu_a * _sigmoid(glu_b)
    mixed = _sigmoid(gs_ref[...].astype(F32)) * ssm + _sigmoid(ga_ref[...].astype(F32)) * attn
    o_ref[...] = mixed.astype(o_ref.dtype)


def gated_merge(y, attn, w_glu, w_ao, proj, col_gs, col_ga, tm, tn, side=()):
    m, ky = y.shape
    ka = attn.shape[1]
    d = w_ao.shape[1]
    nj = d // tn
    grid = (m // tm, nj)
    side_specs, side_shapes = _side_cast_specs(side, grid)
    return pl.pallas_call(
        _with_side_casts(_mix_kernel, 7, 1, len(side)),
        grid=grid,
        in_specs=[pl.BlockSpec((tm, ky), lambda i, j: (i, 0)),
                  pl.BlockSpec((tm, ka), lambda i, j: (i, 0)),
                  pl.BlockSpec((ky, tn), lambda i, j: (0, j)),
                  pl.BlockSpec((ky, tn), lambda i, j: (0, nj + j)),
                  pl.BlockSpec((ka, tn), lambda i, j: (0, j)),
                  pl.BlockSpec((tm, tn), lambda i, j: (i, col_gs // tn + j)),
                  pl.BlockSpec((tm, tn), lambda i, j: (i, col_ga // tn + j))] + side_specs,
        out_specs=[pl.BlockSpec((tm, tn), lambda i, j: (i, j))] + side_specs,
        out_shape=[jax.ShapeDtypeStruct((m, d), BF16)] + side_shapes,
        compiler_params=_params(2),
        name="gated_merge",
    )(y, attn, w_glu, w_glu, w_ao, proj, proj, *[src for src, _ in side])


def _ffn_up_kernel(h_ref, wa_ref, wg_ref, ca_ref, cg_ref, o_ref, o_tail_ref, ua_scr, ug_scr, halo_a, halo_g,
                   *, tm, blocks_per_seq):
    i = pl.program_id(0)
    j = pl.program_id(1)
    last_j = pl.num_programs(1) - 1
    h = h_ref[...]
    seq_start = (i % blocks_per_seq) == 0

    def conv(w_ref, c_ref, u_scr, halo):
        u_scr[pl.ds(HALO_ROWS, tm), :] = jnp.dot(h, w_ref[...].astype(BF16), preferred_element_type=F32)
        u_scr[pl.ds(0, HALO_ROWS), :] = jnp.where(seq_start, 0.0, halo[j])
        halo[j] = u_scr[pl.ds(tm, HALO_ROWS), :]
        cw = c_ref[...]
        return (cw[2:3, :] * u_scr[pl.ds(HALO_ROWS, tm), :]
                + cw[1:2, :] * u_scr[pl.ds(HALO_ROWS - 1, tm), :]
                + cw[0:1, :] * u_scr[pl.ds(HALO_ROWS - 2, tm), :]
                + cw[3:4, :])

    a = conv(wa_ref, ca_ref, ua_scr, halo_a)
    gv = conv(wg_ref, cg_ref, ug_scr, halo_g)
    act = (a * _sigmoid(a) * gv).astype(o_ref.dtype)

    @pl.when(j != last_j)
    def _():
        o_ref[...] = act

    @pl.when(j == last_j)
    def _():
        o_tail_ref[...] = act


def ffn_up(h, w_up, conv_p, seq, tm, tn):
    m, k = h.shape
    f = w_up.shape[1] // 2
    nj = -(-f // tn)
    assert f % LANES == 0 and nj >= 2

    def col0(j, half=0):
        return pl.multiple_of(half * f + jnp.minimum(j * tn, f - tn), LANES)

    lane_tile = pl.Element(tn)
    u_scratch = pltpu.VMEM((tm + HALO_ROWS, tn), F32)
    halo_scratch = pltpu.VMEM((nj, HALO_ROWS, tn), F32)
    return pl.pallas_call(
        functools.partial(_ffn_up_kernel, tm=tm, blocks_per_seq=seq // tm),
        grid=(m // tm, nj),
        in_specs=[pl.BlockSpec((tm, k), lambda i, j: (i, 0)),
                  pl.BlockSpec((pl.Element(k), lane_tile), lambda i, j: (0, col0(j))),
                  pl.BlockSpec((pl.Element(k), lane_tile), lambda i, j: (0, col0(j, 1))),
                  pl.BlockSpec((pl.Element(HALO_ROWS), lane_tile), lambda i, j: (0, col0(j))),
                  pl.BlockSpec((pl.Element(HALO_ROWS), lane_tile), lambda i, j: (0, col0(j, 1)))],
        out_specs=[pl.BlockSpec((tm, tn), lambda i, j: (i, jnp.minimum(j, nj - 2))),
                   pl.BlockSpec((tm, tn), lambda i, j: (i, 0))],
        out_shape=[jax.ShapeDtypeStruct((m, (nj - 1) * tn), BF16),
                   jax.ShapeDtypeStruct((m, tn), BF16)],
        scratch_shapes=[u_scratch, u_scratch, halo_scratch, halo_scratch],
        compiler_params=_params(2),
        name="ffn_up",
    )(h, w_up, w_up, conv_p, conv_p)


def _ffn_down_kernel(a1_ref, a2_ref, w1_ref, w2_ref, r_ref, o_ref):
    o_ref[...] = (r_ref[...]
                  + jnp.dot(a1_ref[...], w1_ref[...], preferred_element_type=F32)
                  + jnp.dot(a2_ref[...], w2_ref[...], preferred_element_type=F32))


def ffn_down(a_main, a_tail, w, res, tm, tn):
    m, k1 = a_main.shape
    k, n = w.shape
    k2 = k - k1
    t_tail = a_tail.shape[1]
    assert 0 < k2 <= t_tail and (t_tail - k2) % k2 == 0 and k1 % k2 == 0
    return pl.pallas_call(
        _ffn_down_kernel,
        grid=(m // tm, n // tn),
        in_specs=[pl.BlockSpec((tm, k1), lambda i, j: (i, 0)),
                  pl.BlockSpec((tm, k2), lambda i, j: (i, (t_tail - k2) // k2)),
                  pl.BlockSpec((k1, tn), lambda i, j: (0, j)),
                  pl.BlockSpec((k2, tn), lambda i, j: (k1 // k2, j)),
                  pl.BlockSpec((tm, tn), lambda i, j: (i, j))],
        out_specs=pl.BlockSpec((tm, tn), lambda i, j: (i, j)),
        out_shape=jax.ShapeDtypeStruct((m, n), F32),
        compiler_params=_params(2),
        name="ffn_down",
    )(a_main, a_tail, w, w, res)


def kernel(x, g_mix, w_in, ssm_log_dt, ssm_a_re, ssm_a_im, ssm_b_re, ssm_b_im, ssm_c_re, ssm_c_im,
           ssm_d, w_glu, w_attn_out, w_out, g_ffn, w_up, conv_w, conv_b, w_down, g_final):
    b, l, d = x.shape
    depth = g_mix.shape[0]
    ssm_width = ssm_a_re.shape[1] * ssm_b_re.shape[-1]
    attn_width = w_attn_out.shape[1]
    n_heads = attn_width // HEAD_DIM
    qkv_width = len(DILATION_PATTERNS) * attn_width
    d_ff = w_down.shape[1]
    ff_tile = 512
    proj_tile = 1024
    merge_tile = 512
    row_tile = 1024
    n_row_tiles = b * l // row_tile
    ssm_tiles = ssm_width // proj_tile
    qkv_tiles = 3 * qkv_width // proj_tile
    gate_tiles = 2 * d // proj_tile

    xf = x.reshape(b * l, d).astype(F32)
    for i in range(depth):
        h = rmsnorm(xf, g_mix[i], BF16)
        w_in_b = w_in[i].astype(BF16)
        n_in_tiles = ssm_tiles + gate_tiles
        proj, w_up_b = matmul(h, w_in_b, BF16, row_tile, proj_tile, n_in_tiles,
                              lambda j: jnp.where(j < ssm_tiles, j, j + qkv_tiles), "in_proj",
                              side=[_cast_chunks(w_up[i], n_row_tiles * n_in_tiles)])
        (qkv,) = qkv_projection(h, w_in_b, ssm_tiles, qkv_width, l, row_tile, proj_tile)
        mats = s5_prepare(ssm_log_dt[i], ssm_a_re[i], ssm_a_im[i], ssm_b_re[i], ssm_b_im[i],
                          ssm_c_re[i], ssm_c_im[i], ssm_d[i])
        y = s5_mixer_gelu(proj, 0, b, l, mats)
        n_attn_steps = b * n_heads * len(DILATION_PATTERNS)
        attn, w_glu_b, w_ao_b, w_out_b = dilated_attention(
            qkv, b, l, n_heads,
            side=[_cast_chunks(w_glu[i], n_attn_steps), _cast_chunks(w_attn_out[i], n_attn_steps),
                  _cast_chunks(w_out[i], n_attn_steps)])
        mixed, w_down_b = gated_merge(y, attn, w_glu_b, w_ao_b, proj, ssm_width, ssm_width + d,
                                      row_tile, merge_tile,
                                      side=[_cast_chunks(w_down[i], n_row_tiles * (d // merge_tile))])
        xf = matmul_residual(mixed, w_out_b, xf, row_tile, 512, 1, "out_proj")

        h = rmsnorm(xf, g_ffn[i], BF16)
        zrow = jnp.zeros((HALO_ROWS - CONV_WIDTH - 1, 2 * d_ff), F32)
        conv_p = jnp.concatenate([conv_w[i].astype(F32), conv_b[i][None].astype(F32), zrow], axis=0)
        act_main, act_tail = ffn_up(h, w_up_b, conv_p, l, row_tile, ff_tile)
        xf = ffn_down(act_main, act_tail, w_down_b, xf, 512, 512)
    out = rmsnorm(xf, g_final, x.dtype)
    return out.reshape(b, l, d)
```

```python
import functools
import math

import jax
import jax.numpy as jnp
from jax import lax
from jax.experimental import pallas as pl
from jax.experimental.pallas import tpu as pltpu

F32 = jnp.float32
BF16 = jnp.bfloat16

RMS_EPS = 1e-5
LANES = 128
HEAD_DIM = 128
ROPE_DIM = HEAD_DIM // 4
ROPE_THETA = 500000.0
DILATION_PATTERNS = ((128, 1), (512, 4), (2048, 16))
ATTN_BLOCK = 128
ATTN_UNROLL = 32
ATTN_ORDER = tuple(sorted(range(len(DILATION_PATTERNS)), key=lambda i: -DILATION_PATTERNS[i][1]))
S5_CHUNK = 16
S5_SLAB_GROUPS = 8
CONV_WIDTH = 3
HALO_ROWS = 8
NEG_BIG = -1e30
VMEM_LIMIT = 56 * 1024 * 1024


def _params(n_axes):
    return pltpu.CompilerParams(dimension_semantics=("arbitrary",) * n_axes,
                                vmem_limit_bytes=VMEM_LIMIT)


def _rmsnorm_kernel(x_ref, g_ref, o_ref):
    x = x_ref[...]
    y = x * lax.rsqrt(jnp.mean(x * x, axis=-1, keepdims=True) + RMS_EPS)
    o_ref[...] = (y * g_ref[...]).astype(o_ref.dtype)


def rmsnorm(x2d, g, out_dtype, tm=512):
    m, d = x2d.shape
    return pl.pallas_call(
        _rmsnorm_kernel,
        grid=(m // tm,),
        in_specs=[pl.BlockSpec((tm, d), lambda i: (i, 0)),
                  pl.BlockSpec((1, d), lambda i: (0, 0))],
        out_specs=pl.BlockSpec((tm, d), lambda i: (i, 0)),
        out_shape=jax.ShapeDtypeStruct((m, d), out_dtype),
        compiler_params=_params(1),
        name="rmsnorm",
    )(x2d, g.reshape(1, d).astype(F32))


def _with_side_casts(body, n_in, n_out, n_side):
    def kernel(*refs):
        ins = refs[:n_in]
        srcs = refs[n_in:n_in + n_side]
        outs = refs[n_in + n_side:n_in + n_side + n_out]
        dsts = refs[n_in + n_side + n_out:n_in + 2 * n_side + n_out]
        rest = refs[n_in + 2 * n_side + n_out:]
        body(*ins, *outs, *rest)
        for src, dst in zip(srcs, dsts):
            dst[...] = src[...].astype(dst.dtype)
    return kernel


def _cast_chunks(w, n_steps):
    bf16_rows = 16
    rows = next(r for r in range(bf16_rows, w.shape[0] + 1, bf16_rows)
                if w.shape[0] % r == 0 and w.shape[0] // r <= n_steps)
    return w, rows


def _side_cast_specs(side, grid):
    n_steps = math.prod(grid)

    def linear_step(*idx):
        s = idx[0]
        for extent, i in zip(grid[1:], idx[1:]):
            s = s * extent + i
        return s

    specs, shapes = [], []
    for src, rows in side:
        n_chunks = src.shape[0] // rows
        assert n_chunks * rows == src.shape[0] and n_chunks <= n_steps
        specs.append(pl.BlockSpec((rows, src.shape[1]),
                                  lambda *idx, n_chunks=n_chunks: (jnp.minimum(linear_step(*idx), n_chunks - 1), 0)))
        shapes.append(jax.ShapeDtypeStruct(src.shape, BF16))
    return specs, shapes


def _mm_kernel(a_ref, w_ref, o_ref):
    o_ref[...] = jnp.dot(a_ref[...], w_ref[...], preferred_element_type=F32).astype(o_ref.dtype)


def matmul(a, w, out_dtype, tm, tn, n_col_tiles, w_col_tile, name, side=()):
    m, k = a.shape
    grid = (m // tm, n_col_tiles)
    side_specs, side_shapes = _side_cast_specs(side, grid)
    return pl.pallas_call(
        _with_side_casts(_mm_kernel, 2, 1, len(side)),
        grid=grid,
        in_specs=[pl.BlockSpec((tm, k), lambda i, j: (i, 0)),
                  pl.BlockSpec((k, tn), lambda i, j: (0, w_col_tile(j)))] + side_specs,
        out_specs=[pl.BlockSpec((tm, tn), lambda i, j: (i, j))] + side_specs,
        out_shape=[jax.ShapeDtypeStruct((m, n_col_tiles * tn), out_dtype)] + side_shapes,
        compiler_params=_params(2),
        name=name,
    )(a, w, *[src for src, _ in side])


def _mm_res_kernel(a_ref, w_ref, r_ref, o_ref):
    k = pl.program_id(2)
    part = jnp.dot(a_ref[...], w_ref[...], preferred_element_type=F32)

    @pl.when(k == 0)
    def _():
        o_ref[...] = r_ref[...] + part

    @pl.when(k != 0)
    def _():
        o_ref[...] += part


def matmul_residual(a, w, res, tm, tn, nk, name, a_single_buffer=False):
    m = a.shape[0]
    k, n = w.shape
    tk = k // nk
    a_mode = pl.Buffered(1) if a_single_buffer else None
    return pl.pallas_call(
        _mm_res_kernel,
        grid=(m // tm, n // tn, nk),
        in_specs=[pl.BlockSpec((tm, tk), lambda i, j, kk: (i, kk), pipeline_mode=a_mode),
                  pl.BlockSpec((tk, tn), lambda i, j, kk: (kk, j)),
                  pl.BlockSpec((tm, tn), lambda i, j, kk: (i, j))],
        out_specs=pl.BlockSpec((tm, tn), lambda i, j, kk: (i, j)),
        out_shape=jax.ShapeDtypeStruct((m, n), F32),
        compiler_params=_params(3),
        name=name,
    )(a, w, res)


def rope_tables(seq):
    half = ROPE_DIM // 2
    inv_freq = ROPE_THETA ** (-jnp.arange(0, ROPE_DIM, 2, dtype=F32) / ROPE_DIM)
    ang = jnp.arange(seq, dtype=F32)[:, None] * inv_freq[None, :]
    cos, sin = jnp.cos(ang), jnp.sin(ang)
    pad = HEAD_DIM - ROPE_DIM
    cos_t = jnp.concatenate([cos, cos, jnp.ones((seq, pad), F32)], axis=1)
    zeros_h = jnp.zeros((seq, half), F32)
    zeros_p = jnp.zeros((seq, pad), F32)
    sin_lo = jnp.concatenate([-sin, zeros_h, zeros_p], axis=1)
    sin_hi = jnp.concatenate([zeros_h, sin, zeros_p], axis=1)
    return cos_t, sin_lo, sin_hi


def _rotary(x, cos_t, sin_lo, sin_hi):
    half = ROPE_DIM // 2
    up = pltpu.roll(x, HEAD_DIM - half, axis=1)
    dn = pltpu.roll(x, half, axis=1)
    return x * cos_t + up * sin_lo + dn * sin_hi


def _qkv_proj_kernel(h_ref, w_ref, cos_ref, slo_ref, shi_ref, o_ref, *, heads_per_tile):
    acc = jnp.dot(h_ref[...], w_ref[...], preferred_element_type=F32)
    cos_t, sin_lo, sin_hi = cos_ref[...], slo_ref[...], shi_ref[...]
    for hh in range(heads_per_tile):
        sl = slice(hh * HEAD_DIM, (hh + 1) * HEAD_DIM)
        o_ref[:, sl] = _rotary(acc[:, sl], cos_t, sin_lo, sin_hi)


def qkv_projection(h, w, col_tile0, qkv_width, seq, tm, tn, side=()):
    m, k = h.shape
    n_q_tiles = qkv_width // tn
    cos_t, sin_lo, sin_hi = rope_tables(seq)
    scale = HEAD_DIM ** -0.5
    zeros = jnp.zeros_like(cos_t)
    tables = (jnp.stack([cos_t * scale, cos_t, jnp.ones_like(cos_t)]),
              jnp.stack([sin_lo * scale, sin_lo, zeros]),
              jnp.stack([sin_hi * scale, sin_hi, zeros]))

    def tab_index(i, j):
        kind = jnp.where(j < n_q_tiles, 0, jnp.where(j < 2 * n_q_tiles, 1, 2))
        return (kind, i % (seq // tm), 0)

    tab_spec = pl.BlockSpec((None, tm, HEAD_DIM), tab_index)
    grid = (m // tm, 3 * n_q_tiles)
    side_specs, side_shapes = _side_cast_specs(side, grid)
    body = functools.partial(_qkv_proj_kernel, heads_per_tile=tn // HEAD_DIM)
    return pl.pallas_call(
        _with_side_casts(body, 5, 1, len(side)),
        grid=grid,
        in_specs=[pl.BlockSpec((tm, k), lambda i, j: (i, 0)),
                  pl.BlockSpec((k, tn), lambda i, j: (0, col_tile0 + j)),
                  tab_spec, tab_spec, tab_spec] + side_specs,
        out_specs=[pl.BlockSpec((tm, tn), lambda i, j: (i, j))] + side_specs,
        out_shape=[jax.ShapeDtypeStruct((m, 3 * qkv_width), F32)] + side_shapes,
        compiler_params=_params(2),
        name="qkv_proj",
    )(h, w, *tables, *[src for src, _ in side])


def _gelu_tanh(x):
    c = math.sqrt(2.0 / math.pi)
    return 0.5 * x * (1.0 + jnp.tanh(c * (x + 0.044715 * (x * x * x))))


def _split_bf16(x):
    hi = x.astype(BF16)
    return hi, (x - hi.astype(F32)).astype(BF16)


def _s5_kernel(u_ref, bbn_ref, cl_ref, dsk_ref, wc_ref, vc_ref, a_ref, o_ref,
               xf_scr, bc_scr, sin_scr, m_scr, w_scr, vt_scr, *, rows, n_seq):
    t_ = S5_CHUNK
    half = bc_scr.shape[1] // 2
    n_pair = half // LANES
    nt = (((1,), (1,)), ((), ()))

    @pl.when(pl.program_id(1) == 0)
    def _():
        b_hi, b_lo = _split_bf16(bbn_ref[...])
        c_hi, c_lo = _split_bf16(cl_ref[...])
        kt = (lax.dot_general(b_hi, c_hi, nt, preferred_element_type=F32)
              + lax.dot_general(b_hi, c_lo, nt, preferred_element_type=F32)
              + lax.dot_general(b_lo, c_hi, nt, preferred_element_type=F32))
        brow = lax.broadcasted_iota(jnp.int32, (LANES, LANES), 0)
        bcol = lax.broadcasted_iota(jnp.int32, (LANES, LANES), 1)
        chan = LANES // S5_SLAB_GROUPS
        same_group = (brow // chan) == (bcol // chan)
        zeros = jnp.zeros((LANES, LANES), BF16)
        for lag in range(t_):
            blk = jnp.where(same_group, kt[:, lag * LANES:(lag + 1) * LANES], 0.0)
            if lag == 0:
                blk = blk + jnp.where(brow == bcol, dsk_ref[...], 0.0)
            blk = blk.astype(BF16)
            for tin in range(t_ - lag):
                tout = tin + lag
                m_scr[tin * LANES:(tin + 1) * LANES, tout * LANES:(tout + 1) * LANES] = blk
        for tout in range(0, t_, 2):
            m_scr[(tout + 1) * LANES:(tout + 2) * LANES, tout * LANES:(tout + 1) * LANES] = zeros
        kdim = m_scr.shape[0]
        row_group = (lax.broadcasted_iota(jnp.int32, (kdim, LANES), 0) // (LANES // S5_SLAB_GROUPS)) % S5_SLAB_GROUPS
        lane_half = lax.broadcasted_iota(jnp.int32, (kdim, LANES), 1) // (LANES // 2)
        diff = row_group - lane_half
        for ri in range(2):
            wv = wc_ref[ri]
            vv = vc_ref[ri]
            for k in range(n_pair):
                sel = diff == 2 * k
                cols = slice(ri * half + k * LANES, ri * half + (k + 1) * LANES)
                w_scr[:, cols] = jnp.where(sel, wv, 0.0).astype(BF16)
                vt_scr[:, cols] = jnp.where(sel, vv, 0.0).astype(BF16)

    all_rows = rows * n_seq
    xf_scr[...] = u_ref[...].astype(F32)
    planes = [xf_scr[pl.ds(t, all_rows, stride=t_), :].astype(BF16) for t in range(t_)]
    u = jnp.concatenate(planes, axis=1)
    bc_scr[...] = jnp.dot(u, w_scr[...], preferred_element_type=F32)

    ar = a_ref[0:1, :]
    ai = a_ref[1:2, :]

    def tile_body(i, carry):
        new_carry = []
        for q in range(n_seq):
            sr, si = carry[q]
            base = pl.multiple_of(q * rows + i * 8, 8)
            tile = bc_scr[pl.ds(base, 8), :]
            rows_r, rows_i = [], []
            for r in range(8):
                rows_r.append(sr)
                rows_i.append(si)
                br = tile[r:r + 1, :half]
                bi = tile[r:r + 1, half:]
                sr, si = ar * sr - ai * si + br, ar * si + ai * sr + bi
            sin_scr[pl.ds(base, 8), :] = jnp.concatenate(
                [jnp.concatenate(rows_r, axis=0), jnp.concatenate(rows_i, axis=0)], axis=1)
            new_carry.append((sr, si))
        return tuple(new_carry)

    zero = jnp.zeros((1, half), F32)
    lax.fori_loop(0, rows // 8, tile_body, ((zero, zero),) * n_seq)

    s_in = sin_scr[...].astype(BF16)
    tile_w = 2 * LANES
    for jt in range(t_ // 2):
        k_hi = (2 * jt + 2) * LANES
        cols = slice(jt * tile_w, (jt + 1) * tile_w)
        y = (jnp.dot(u[:, :k_hi], m_scr[0:k_hi, cols], preferred_element_type=F32)
             + lax.dot_general(s_in, vt_scr[cols, :], nt, preferred_element_type=F32))
        y = _gelu_tanh(y)
        for tt in range(2):
            xf_scr[pl.ds(2 * jt + tt, all_rows, stride=t_), :] = y[:, tt * LANES:(tt + 1) * LANES]
    o_ref[...] = xf_scr[...].astype(o_ref.dtype)


def s5_prepare(log_dt, a_re, a_im, b_re, b_im, c_re, c_im, d_skip):
    g, p = a_re.shape
    c = b_re.shape[-1]
    t_ = S5_CHUNK
    gl = S5_SLAB_GROUPS
    ns = g // gl
    assert gl * c == LANES and 2 * p == LANES
    def twice(x):
        x = x.astype(F32)
        return jnp.concatenate([x, x], axis=-1)

    a_re, a_im = twice(a_re), twice(a_im)
    dt = jnp.exp(log_dt.astype(F32))[:, None]
    ks = jnp.arange(t_ + 1, dtype=F32)[:, None, None]
    mag = jnp.exp(ks * (dt * a_re)[None])
    ang = ks * (dt * a_im)[None]
    pr, pi = mag * jnp.cos(ang), mag * jnp.sin(ang)
    lb_re, lb_im = pr[1], pi[1]
    den = a_re * a_re + a_im * a_im
    f_re = ((lb_re - 1.0) * a_re + lb_im * a_im) / den
    f_im = (lb_im * a_re - (lb_re - 1.0) * a_im) / den
    br = twice(jnp.swapaxes(b_re, 1, 2))
    bi = twice(jnp.swapaxes(b_im, 1, 2))
    bb_re = f_re[:, None, :] * br - f_im[:, None, :] * bi
    bb_im = f_re[:, None, :] * bi + f_im[:, None, :] * br
    cr, ci = twice(c_re), twice(c_im)
    cp_re = cr[None] * pr[:, :, None, :] - ci[None] * pi[:, :, None, :]
    cp_im = cr[None] * pi[:, :, None, :] + ci[None] * pr[:, :, None, :]
    prr = pr[t_ - 1::-1][:t_]
    pir = pi[t_ - 1::-1][:t_]
    w_re = prr[:, :, None, :] * bb_re[None] - pir[:, :, None, :] * bb_im[None]
    w_im = prr[:, :, None, :] * bb_im[None] + pir[:, :, None, :] * bb_re[None]

    def rows_tgc(x):
        return x.reshape(t_, ns, gl * c, 2 * p).transpose(1, 0, 2, 3).reshape(ns, t_ * gl * c, 2 * p)

    first_copy = jnp.arange(2 * p) < p
    bbn = jnp.where(first_copy, bb_re, -bb_im).reshape(ns, gl * c, 2 * p)
    cl = rows_tgc(jnp.where(first_copy, cp_re[:t_], cp_im[:t_]))
    dsk = d_skip.astype(F32).reshape(ns, 1, gl * c)
    wc = jnp.stack([rows_tgc(w_re), rows_tgc(w_im)], axis=1)
    vc = jnp.stack([rows_tgc(cp_re[1:]), rows_tgc(-cp_im[1:])], axis=1)
    a_mat = jnp.stack([pr[t_][:, :p], pi[t_][:, :p]], axis=0).reshape(2, ns, gl * p).transpose(1, 0, 2)
    return bbn, cl, dsk, wc, vc, a_mat


def s5_mixer_gelu(proj, col_block0, n_batch, seq, mats):
    bbn, cl, dsk, wc, vc, a_mat = mats
    ns = bbn.shape[0]
    kdim = wc.shape[2]
    sdim = 2 * a_mat.shape[2]
    rows = seq // S5_CHUNK
    n_seq = 2 if n_batch % 2 == 0 else 1
    return pl.pallas_call(
        functools.partial(_s5_kernel, rows=rows, n_seq=n_seq),
        grid=(ns, n_batch // n_seq),
        in_specs=[pl.BlockSpec((n_seq * seq, LANES), lambda s, b: (b, col_block0 + s)),
                  pl.BlockSpec((None, LANES, LANES), lambda s, b: (s, 0, 0)),
                  pl.BlockSpec((None, kdim, LANES), lambda s, b: (s, 0, 0)),
                  pl.BlockSpec((None, 1, LANES), lambda s, b: (s, 0, 0)),
                  pl.BlockSpec((None, 2, kdim, LANES), lambda s, b: (s, 0, 0, 0)),
                  pl.BlockSpec((None, 2, kdim, LANES), lambda s, b: (s, 0, 0, 0)),
                  pl.BlockSpec((None, 2, sdim // 2), lambda s, b: (s, 0, 0))],
        out_specs=pl.BlockSpec((n_seq * seq, LANES), lambda s, b: (b, s)),
        out_shape=jax.ShapeDtypeStruct((n_batch * seq, ns * LANES), BF16),
        scratch_shapes=[pltpu.VMEM((n_seq * seq, LANES), F32),
                        pltpu.VMEM((n_seq * rows, sdim), F32),
                        pltpu.VMEM((n_seq * rows, sdim), F32),
                        pltpu.VMEM((kdim, kdim), BF16),
                        pltpu.VMEM((kdim, sdim), BF16),
                        pltpu.VMEM((kdim, sdim), BF16)],
        compiler_params=_params(2),
        name="s5_mixer",
    )(proj, bbn, cl, dsk, wc, vc, a_mat)


def _attn_kernel(q_ref, k_ref, v_ref, o_ref, acc_scr, m_scr, l_scr, *, seq):
    step = pl.program_id(2)
    n_g = len(DILATION_PATTERNS)
    blk = ATTN_BLOCK
    row = lax.broadcasted_iota(jnp.int32, (blk, blk), 0)
    col = lax.broadcasted_iota(jnp.int32, (blk, blk), 1)
    cur_ok = col <= row
    prev_ok = col >= row
    nt = (((1,), (1,)), ((), ()))
    ones = jnp.ones((blk, HEAD_DIM), BF16)

    def run_group(step_i, dil):
        first = step_i == 0
        last = step_i == n_g - 1
        n_qb = seq // dil // blk

        def load(ref, start):
            if dil == 1:
                return ref[pl.ds(start, blk), :]
            return ref[pl.ds(start, blk, stride=dil), :]

        def store(ref, start, val):
            if dil == 1:
                ref[pl.ds(start, blk), :] = val
            else:
                ref[pl.ds(start, blk, stride=dil), :] = val

        def block(base, prev, prev_cond):
            q = load(q_ref, base).astype(BF16)
            kc = load(k_ref, base).astype(BF16)
            vc = jnp.concatenate([load(v_ref, base).astype(BF16), ones], axis=1)
            s_c = jnp.where(cur_ok, lax.dot_general(q, kc, nt, preferred_element_type=F32), NEG_BIG)
            m_blk = jnp.max(s_c, axis=1, keepdims=True)
            if prev is not None:
                ok = prev_ok if prev_cond is None else jnp.logical_and(prev_ok, prev_cond)
                s_p = jnp.where(ok, lax.dot_general(q, prev[0], nt, preferred_element_type=F32), NEG_BIG)
                m_blk = jnp.maximum(m_blk, jnp.max(s_p, axis=1, keepdims=True))
            if first:
                m_new = m_blk
            else:
                m_old = load(m_scr, base)
                m_new = jnp.maximum(m_old, m_blk)
            pv = jnp.dot(jnp.exp(s_c - m_new).astype(BF16), vc, preferred_element_type=F32)
            if prev is not None:
                pv = pv + jnp.dot(jnp.exp(s_p - m_new).astype(BF16), prev[1], preferred_element_type=F32)
            acc_new = pv[:, :HEAD_DIM]
            l_new = pv[:, HEAD_DIM:]
            if not first:
                alpha = jnp.exp(m_old - m_new)
                acc_new = alpha * load(acc_scr, base) + acc_new
                l_new = alpha * load(l_scr, base) + l_new
            if last and dil == 1:
                o_ref[pl.ds(pl.multiple_of(base, blk), blk), :] = (acc_new / l_new).astype(o_ref.dtype)
            else:
                store(acc_scr, base, acc_new)
                store(l_scr, base, l_new)
                if not last:
                    store(m_scr, base, jnp.broadcast_to(m_new, (blk, blk)))
            return kc, vc

        if n_qb >= ATTN_UNROLL:
            assert n_qb % ATTN_UNROLL == 0

            def res_body(res, _):
                def body(it, carry):
                    prev = carry
                    for u in range(ATTN_UNROLL):
                        base = (it * ATTN_UNROLL + u) * (blk * dil) + res
                        prev = block(base, prev, (it > 0) if u == 0 else None)
                    return prev
                init = (jnp.zeros((blk, HEAD_DIM), BF16), jnp.zeros((blk, 2 * HEAD_DIM), BF16))
                lax.fori_loop(0, n_qb // ATTN_UNROLL, body, init)
                return 0
            if dil == 1:
                res_body(0, 0)
            else:
                lax.fori_loop(0, dil, res_body, 0)
        else:
            assert ATTN_UNROLL % n_qb == 0
            res_per_body = ATTN_UNROLL // n_qb
            assert dil % res_per_body == 0

            def body(it, _):
                for rr in range(res_per_body):
                    res = it * res_per_body + rr
                    prev = None
                    for u in range(n_qb):
                        prev = block(u * (blk * dil) + res, prev, None)
                return 0
            lax.fori_loop(0, dil // res_per_body, body, 0)

    for step_i, gi in enumerate(ATTN_ORDER):
        window, dil = DILATION_PATTERNS[gi]
        assert window // dil == blk
        pl.when(step == step_i)(functools.partial(run_group, step_i, dil))

    if DILATION_PATTERNS[ATTN_ORDER[-1]][1] != 1:
        @pl.when(step == n_g - 1)
        def _():
            o_ref[...] = (acc_scr[...] / l_scr[...]).astype(o_ref.dtype)


def dilated_attention(qkv, n_batch, seq, n_heads, side=()):
    n_g = len(DILATION_PATTERNS)
    width = n_g * n_heads

    def group_of_step(s):
        gi = ATTN_ORDER[-1]
        for step_i in range(n_g - 2, -1, -1):
            gi = jnp.where(s == step_i, ATTN_ORDER[step_i], gi)
        return gi

    def col_spec(part):
        return pl.BlockSpec((seq, LANES), lambda b, h, s: (b, part * width + group_of_step(s) * n_heads + h))

    grid = (n_batch, n_heads, n_g)
    side_specs, side_shapes = _side_cast_specs(side, grid)
    return pl.pallas_call(
        _with_side_casts(functools.partial(_attn_kernel, seq=seq), 3, 1, len(side)),
        grid=grid,
        in_specs=[col_spec(0), col_spec(1), col_spec(2)] + side_specs,
        out_specs=[pl.BlockSpec((seq, LANES), lambda b, h, g: (b, h))] + side_specs,
        out_shape=[jax.ShapeDtypeStruct((n_batch * seq, n_heads * LANES), BF16)] + side_shapes,
        scratch_shapes=[pltpu.VMEM((seq, LANES), F32)] * 3,
        compiler_params=_params(3),
        name="dilated_attention",
    )(qkv, qkv, qkv, *[src for src, _ in side])


def _sigmoid(x):
    return 1.0 / (1.0 + jnp.exp(-x))


def _mix_kernel(y_ref, at_ref, wa_ref, wb_ref, wo_ref, gs_ref, ga_ref, o_ref):
    y = y_ref[...]
    glu_a = jnp.dot(y, wa_ref[...], preferred_element_type=F32)
    glu_b = jnp.dot(y, wb_ref[...], preferred_element_type=F32)
    attn = jnp.dot(at_ref[...], wo_ref[...], preferred_element_type=F32)
    ssm = glu_a * _sigmoid(glu_b)
    mixed = _sigmoid(gs_ref[...].astype(F32)) * ssm + _sigmoid(ga_ref[...].astype(F32)) * attn
    o_ref[...] = mixed.astype(o_ref.dtype)


def gated_merge(y, attn, w_glu, w_ao, proj, col_gs, col_ga, tm, tn, side=()):
    m, ky = y.shape
    ka = attn.shape[1]
    d = w_ao.shape[1]
    nj = d // tn
    grid = (m // tm, nj)
    side_specs, side_shapes = _side_cast_specs(side, grid)
    return pl.pallas_call(
        _with_side_casts(_mix_kernel, 7, 1, len(side)),
        grid=grid,
        in_specs=[pl.BlockSpec((tm, ky), lambda i, j: (i, 0)),
                  pl.BlockSpec((tm, ka), lambda i, j: (i, 0)),
                  pl.BlockSpec((ky, tn), lambda i, j: (0, j)),
                  pl.BlockSpec((ky, tn), lambda i, j: (0, nj + j)),
                  pl.BlockSpec((ka, tn), lambda i, j: (0, j)),
                  pl.BlockSpec((tm, tn), lambda i, j: (i, col_gs // tn + j)),
                  pl.BlockSpec((tm, tn), lambda i, j: (i, col_ga // tn + j))] + side_specs,
        out_specs=[pl.BlockSpec((tm, tn), lambda i, j: (i, j))] + side_specs,
        out_shape=[jax.ShapeDtypeStruct((m, d), BF16)] + side_shapes,
        compiler_params=_params(2),
        name="gated_merge",
    )(y, attn, w_glu, w_glu, w_ao, proj, proj, *[src for src, _ in side])


def _ffn_up_kernel(h_ref, wa_ref, wg_ref, ca_ref, cg_ref, o_ref, ua_scr, ug_scr, halo_a, halo_g,
                   *, tm, blocks_per_seq):
    i = pl.program_id(0)
    j = pl.program_id(1)
    h = h_ref[...]
    seq_start = (i % blocks_per_seq) == 0

    def conv(w_ref, c_ref, u_scr, halo):
        u_scr[pl.ds(HALO_ROWS, tm), :] = jnp.dot(h, w_ref[...].astype(BF16), preferred_element_type=F32)
        u_scr[pl.ds(0, HALO_ROWS), :] = jnp.where(seq_start, 0.0, halo[j])
        halo[j] = u_scr[pl.ds(tm, HALO_ROWS), :]
        cw = c_ref[...]
        return (cw[2:3, :] * u_scr[pl.ds(HALO_ROWS, tm), :]
                + cw[1:2, :] * u_scr[pl.ds(HALO_ROWS - 1, tm), :]
                + cw[0:1, :] * u_scr[pl.ds(HALO_ROWS - 2, tm), :]
                + cw[3:4, :])

    a = conv(wa_ref, ca_ref, ua_scr, halo_a)
    gv = conv(wg_ref, cg_ref, ug_scr, halo_g)
    o_ref[...] = (a * _sigmoid(a) * gv).astype(o_ref.dtype)


def ffn_up(h, w_up, conv_p, seq, tm, tn):
    m, k = h.shape
    f = w_up.shape[1] // 2
    nj = -(-f // tn)
    assert f % LANES == 0 and nj >= 2

    def col0(j, half=0):
        return pl.multiple_of(half * f + jnp.minimum(j * tn, f - tn), LANES)

    lane_tile = pl.Element(tn)
    u_scratch = pltpu.VMEM((tm + HALO_ROWS, tn), F32)
    halo_scratch = pltpu.VMEM((nj, HALO_ROWS, tn), F32)
    return pl.pallas_call(
        functools.partial(_ffn_up_kernel, tm=tm, blocks_per_seq=seq // tm),
        grid=(m // tm, nj),
        in_specs=[pl.BlockSpec((tm, k), lambda i, j: (i, 0)),
                  pl.BlockSpec((pl.Element(k), lane_tile), lambda i, j: (0, col0(j))),
                  pl.BlockSpec((pl.Element(k), lane_tile), lambda i, j: (0, col0(j, 1))),
                  pl.BlockSpec((pl.Element(HALO_ROWS), lane_tile), lambda i, j: (0, col0(j))),
                  pl.BlockSpec((pl.Element(HALO_ROWS), lane_tile), lambda i, j: (0, col0(j, 1)))],
        out_specs=pl.BlockSpec((tm, tn), lambda i, j: (i, j)),
        out_shape=jax.ShapeDtypeStruct((m, nj * tn), BF16),
        scratch_shapes=[u_scratch, u_scratch, halo_scratch, halo_scratch],
        compiler_params=_params(2),
        name="ffn_up",
    )(h, w_up, w_up, conv_p, conv_p)


def _ffn_down_kernel(a1_ref, a2_ref, w1_ref, w2_ref, r_ref, o_ref):
    o_ref[...] = (r_ref[...]
                  + jnp.dot(a1_ref[...], w1_ref[...], preferred_element_type=F32)
                  + jnp.dot(a2_ref[...], w2_ref[...], preferred_element_type=F32))


def ffn_down(act, up_tile, w, res, tm, tn):
    m, width = act.shape
    k, n = w.shape
    k1 = width - up_tile
    k2 = k - k1
    assert 0 < k2 <= up_tile and width % k2 == 0 and k1 % k2 == 0
    return pl.pallas_call(
        _ffn_down_kernel,
        grid=(m // tm, n // tn),
        in_specs=[pl.BlockSpec((tm, k1), lambda i, j: (i, 0)),
                  pl.BlockSpec((tm, k2), lambda i, j: (i, width // k2 - 1)),
                  pl.BlockSpec((k1, tn), lambda i, j: (0, j)),
                  pl.BlockSpec((k2, tn), lambda i, j: (k1 // k2, j)),
                  pl.BlockSpec((tm, tn), lambda i, j: (i, j))],
        out_specs=pl.BlockSpec((tm, tn), lambda i, j: (i, j)),
        out_shape=jax.ShapeDtypeStruct((m, n), F32),
        compiler_params=_params(2),
        name="ffn_down",
    )(act, act, w, w, res)


def kernel(x, g_mix, w_in, ssm_log_dt, ssm_a_re, ssm_a_im, ssm_b_re, ssm_b_im, ssm_c_re, ssm_c_im,
           ssm_d, w_glu, w_attn_out, w_out, g_ffn, w_up, conv_w, conv_b, w_down, g_final):
    b, l, d = x.shape
    depth = g_mix.shape[0]
    ssm_width = ssm_a_re.shape[1] * ssm_b_re.shape[-1]
    attn_width = w_attn_out.shape[1]
    n_heads = attn_width // HEAD_DIM
    qkv_width = len(DILATION_PATTERNS) * attn_width
    d_ff = w_down.shape[1]
    ff_tile = 512
    proj_tile = 1024
    merge_tile = 512
    row_tile = 1024
    n_row_tiles = b * l // row_tile
    ssm_tiles = ssm_width // proj_tile
    qkv_tiles = 3 * qkv_width // proj_tile
    gate_tiles = 2 * d // proj_tile

    xf = x.reshape(b * l, d).astype(F32)
    for i in range(depth):
        h = rmsnorm(xf, g_mix[i], BF16)
        w_in_b = w_in[i].astype(BF16)
        n_in_tiles = ssm_tiles + gate_tiles
        proj, w_up_b = matmul(h, w_in_b, BF16, row_tile, proj_tile, n_in_tiles,
                              lambda j: jnp.where(j < ssm_tiles, j, j + qkv_tiles), "in_proj",
                              side=[_cast_chunks(w_up[i], n_row_tiles * n_in_tiles)])
        (qkv,) = qkv_projection(h, w_in_b, ssm_tiles, qkv_width, l, row_tile, proj_tile)
        mats = s5_prepare(ssm_log_dt[i], ssm_a_re[i], ssm_a_im[i], ssm_b_re[i], ssm_b_im[i],
                          ssm_c_re[i], ssm_c_im[i], ssm_d[i])
        y = s5_mixer_gelu(proj, 0, b, l, mats)
        n_attn_steps = b * n_heads * len(DILATION_PATTERNS)
        attn, w_glu_b, w_ao_b, w_out_b = dilated_attention(
            qkv, b, l, n_heads,
            side=[_cast_chunks(w_glu[i], n_attn_steps), _cast_chunks(w_attn_out[i], n_attn_steps),
                  _cast_chunks(w_out[i], n_attn_steps)])
        mixed, w_down_b = gated_merge(y, attn, w_glu_b, w_ao_b, proj, ssm_width, ssm_width + d,
                                      row_tile, merge_tile,
                                      side=[_cast_chunks(w_down[i], n_row_tiles * (d // merge_tile))])
        xf = matmul_residual(mixed, w_out_b, xf, row_tile, 512, 1, "out_proj")

        h = rmsnorm(xf, g_ffn[i], BF16)
        zrow = jnp.zeros((HALO_ROWS - CONV_WIDTH - 1, 2 * d_ff), F32)
        conv_p = jnp.concatenate([conv_w[i].astype(F32), conv_b[i][None].astype(F32), zrow], axis=0)
        act = ffn_up(h, w_up_b, conv_p, l, row_tile, ff_tile)
        xf = ffn_down(act, ff_tile, w_down_b, xf, 512, 512)
    out = rmsnorm(xf, g_final, x.dtype)
    return out.reshape(b, l, d)
```

```python
import functools
import math

import jax
import jax.numpy as jnp
from jax import lax
from jax.experimental import pallas as pl
from jax.experimental.pallas import tpu as pltpu

F32 = jnp.float32
BF16 = jnp.bfloat16

RMS_EPS = 1e-5
LANES = 128
HEAD_DIM = 128
ROPE_DIM = HEAD_DIM // 4
ROPE_THETA = 500000.0
DILATION_PATTERNS = ((128, 1), (512, 4), (2048, 16))
ATTN_BLOCK = 128
ATTN_UNROLL = 32
ATTN_ORDER = tuple(sorted(range(len(DILATION_PATTERNS)), key=lambda i: -DILATION_PATTERNS[i][1]))
S5_CHUNK = 16
S5_SLAB_GROUPS = 8
CONV_WIDTH = 3
HALO_ROWS = 8
NEG_BIG = -1e30
VMEM_LIMIT = 56 * 1024 * 1024


def _params(n_axes):
    return pltpu.CompilerParams(dimension_semantics=("arbitrary",) * n_axes,
                                vmem_limit_bytes=VMEM_LIMIT)


def _rmsnorm_kernel(x_ref, g_ref, o_ref):
    x = x_ref[...]
    y = x * lax.rsqrt(jnp.mean(x * x, axis=-1, keepdims=True) + RMS_EPS)
    o_ref[...] = (y * g_ref[...]).astype(o_ref.dtype)


def rmsnorm(x2d, g, out_dtype, tm=512):
    m, d = x2d.shape
    return pl.pallas_call(
        _rmsnorm_kernel,
        grid=(m // tm,),
        in_specs=[pl.BlockSpec((tm, d), lambda i: (i, 0)),
                  pl.BlockSpec((1, d), lambda i: (0, 0))],
        out_specs=pl.BlockSpec((tm, d), lambda i: (i, 0)),
        out_shape=jax.ShapeDtypeStruct((m, d), out_dtype),
        compiler_params=_params(1),
        name="rmsnorm",
    )(x2d, g.reshape(1, d).astype(F32))


def _with_side_casts(body, n_in, n_out, n_side):
    def kernel(*refs):
        ins = refs[:n_in]
        srcs = refs[n_in:n_in + n_side]
        outs = refs[n_in + n_side:n_in + n_side + n_out]
        dsts = refs[n_in + n_side + n_out:n_in + 2 * n_side + n_out]
        rest = refs[n_in + 2 * n_side + n_out:]
        body(*ins, *outs, *rest)
        for src, dst in zip(srcs, dsts):
            dst[...] = src[...].astype(dst.dtype)
    return kernel


def _cast_chunks(w, n_steps):
    bf16_rows = 16
    rows = next(r for r in range(bf16_rows, w.shape[0] + 1, bf16_rows)
                if w.shape[0] % r == 0 and w.shape[0] // r <= n_steps)
    return w, rows


def _side_cast_specs(side, grid):
    n_steps = math.prod(grid)

    def linear_step(*idx):
        s = idx[0]
        for extent, i in zip(grid[1:], idx[1:]):
            s = s * extent + i
        return s

    specs, shapes = [], []
    for src, rows in side:
        n_chunks = src.shape[0] // rows
        assert n_chunks * rows == src.shape[0] and n_chunks <= n_steps
        specs.append(pl.BlockSpec((rows, src.shape[1]),
                                  lambda *idx, n_chunks=n_chunks: (jnp.minimum(linear_step(*idx), n_chunks - 1), 0)))
        shapes.append(jax.ShapeDtypeStruct(src.shape, BF16))
    return specs, shapes


def _mm_kernel(a_ref, w_ref, o_ref):
    o_ref[...] = jnp.dot(a_ref[...], w_ref[...], preferred_element_type=F32).astype(o_ref.dtype)


def matmul(a, w, out_dtype, tm, tn, n_col_tiles, w_col_tile, name, side=()):
    m, k = a.shape
    grid = (m // tm, n_col_tiles)
    side_specs, side_shapes = _side_cast_specs(side, grid)
    return pl.pallas_call(
        _with_side_casts(_mm_kernel, 2, 1, len(side)),
        grid=grid,
        in_specs=[pl.BlockSpec((tm, k), lambda i, j: (i, 0)),
                  pl.BlockSpec((k, tn), lambda i, j: (0, w_col_tile(j)))] + side_specs,
        out_specs=[pl.BlockSpec((tm, tn), lambda i, j: (i, j))] + side_specs,
        out_shape=[jax.ShapeDtypeStruct((m, n_col_tiles * tn), out_dtype)] + side_shapes,
        compiler_params=_params(2),
        name=name,
    )(a, w, *[src for src, _ in side])


def _mm_res_kernel(a_ref, w_ref, r_ref, o_ref):
    k = pl.program_id(2)
    part = jnp.dot(a_ref[...], w_ref[...], preferred_element_type=F32)

    @pl.when(k == 0)
    def _():
        o_ref[...] = r_ref[...] + part

    @pl.when(k != 0)
    def _():
        o_ref[...] += part


def matmul_residual(a, w, res, tm, tn, nk, name, a_single_buffer=False):
    m = a.shape[0]
    k, n = w.shape
    tk = k // nk
    a_mode = pl.Buffered(1) if a_single_buffer else None
    return pl.pallas_call(
        _mm_res_kernel,
        grid=(m // tm, n // tn, nk),
        in_specs=[pl.BlockSpec((tm, tk), lambda i, j, kk: (i, kk), pipeline_mode=a_mode),
                  pl.BlockSpec((tk, tn), lambda i, j, kk: (kk, j)),
                  pl.BlockSpec((tm, tn), lambda i, j, kk: (i, j))],
        out_specs=pl.BlockSpec((tm, tn), lambda i, j, kk: (i, j)),
        out_shape=jax.ShapeDtypeStruct((m, n), F32),
        compiler_params=_params(3),
        name=name,
    )(a, w, res)


def rope_tables(seq):
    half = ROPE_DIM // 2
    inv_freq = ROPE_THETA ** (-jnp.arange(0, ROPE_DIM, 2, dtype=F32) / ROPE_DIM)
    ang = jnp.arange(seq, dtype=F32)[:, None] * inv_freq[None, :]
    cos, sin = jnp.cos(ang), jnp.sin(ang)
    pad = HEAD_DIM - ROPE_DIM
    cos_t = jnp.concatenate([cos, cos, jnp.ones((seq, pad), F32)], axis=1)
    zeros_h = jnp.zeros((seq, half), F32)
    zeros_p = jnp.zeros((seq, pad), F32)
    sin_lo = jnp.concatenate([-sin, zeros_h, zeros_p], axis=1)
    sin_hi = jnp.concatenate([zeros_h, sin, zeros_p], axis=1)
    return cos_t, sin_lo, sin_hi


def _rotary(x, cos_t, sin_lo, sin_hi):
    half = ROPE_DIM // 2
    up = pltpu.roll(x, HEAD_DIM - half, axis=1)
    dn = pltpu.roll(x, half, axis=1)
    return x * cos_t + up * sin_lo + dn * sin_hi


def _qkv_proj_kernel(h_ref, w_ref, cos_ref, slo_ref, shi_ref, o_ref, *, heads_per_tile):
    acc = jnp.dot(h_ref[...], w_ref[...], preferred_element_type=F32)
    cos_t, sin_lo, sin_hi = cos_ref[...], slo_ref[...], shi_ref[...]
    for hh in range(heads_per_tile):
        sl = slice(hh * HEAD_DIM, (hh + 1) * HEAD_DIM)
        o_ref[:, sl] = _rotary(acc[:, sl], cos_t, sin_lo, sin_hi)


def qk_projection(h, w, col_tile0, qkv_width, seq, tm, tn, side=()):
    m, k = h.shape
    n_q_tiles = qkv_width // tn
    cos_t, sin_lo, sin_hi = rope_tables(seq)
    scale = HEAD_DIM ** -0.5
    tables = (jnp.stack([cos_t * scale, cos_t]),
              jnp.stack([sin_lo * scale, sin_lo]),
              jnp.stack([sin_hi * scale, sin_hi]))

    def tab_index(i, j):
        return (jnp.where(j < n_q_tiles, 0, 1), i % (seq // tm), 0)

    tab_spec = pl.BlockSpec((None, tm, HEAD_DIM), tab_index)
    grid = (m // tm, 2 * n_q_tiles)
    side_specs, side_shapes = _side_cast_specs(side, grid)
    body = functools.partial(_qkv_proj_kernel, heads_per_tile=tn // HEAD_DIM)
    return pl.pallas_call(
        _with_side_casts(body, 5, 1, len(side)),
        grid=grid,
        in_specs=[pl.BlockSpec((tm, k), lambda i, j: (i, 0)),
                  pl.BlockSpec((k, tn), lambda i, j: (0, col_tile0 + j)),
                  tab_spec, tab_spec, tab_spec] + side_specs,
        out_specs=[pl.BlockSpec((tm, tn), lambda i, j: (i, j))] + side_specs,
        out_shape=[jax.ShapeDtypeStruct((m, 2 * qkv_width), F32)] + side_shapes,
        compiler_params=_params(2),
        name="qk_proj",
    )(h, w, *tables, *[src for src, _ in side])


def _gelu_tanh(x):
    c = math.sqrt(2.0 / math.pi)
    return 0.5 * x * (1.0 + jnp.tanh(c * (x + 0.044715 * (x * x * x))))


def _split_bf16(x):
    hi = x.astype(BF16)
    return hi, (x - hi.astype(F32)).astype(BF16)


def _s5_kernel(u_ref, bbn_ref, cl_ref, dsk_ref, wc_ref, vc_ref, a_ref, o_ref,
               xf_scr, bc_scr, sin_scr, m_scr, w_scr, vt_scr, *, rows, n_seq):
    t_ = S5_CHUNK
    half = bc_scr.shape[1] // 2
    n_pair = half // LANES
    nt = (((1,), (1,)), ((), ()))

    @pl.when(pl.program_id(1) == 0)
    def _():
        b_hi, b_lo = _split_bf16(bbn_ref[...])
        c_hi, c_lo = _split_bf16(cl_ref[...])
        kt = (lax.dot_general(b_hi, c_hi, nt, preferred_element_type=F32)
              + lax.dot_general(b_hi, c_lo, nt, preferred_element_type=F32)
              + lax.dot_general(b_lo, c_hi, nt, preferred_element_type=F32))
        brow = lax.broadcasted_iota(jnp.int32, (LANES, LANES), 0)
        bcol = lax.broadcasted_iota(jnp.int32, (LANES, LANES), 1)
        chan = LANES // S5_SLAB_GROUPS
        same_group = (brow // chan) == (bcol // chan)
        zeros = jnp.zeros((LANES, LANES), BF16)
        for lag in range(t_):
            blk = jnp.where(same_group, kt[:, lag * LANES:(lag + 1) * LANES], 0.0)
            if lag == 0:
                blk = blk + jnp.where(brow == bcol, dsk_ref[...], 0.0)
            blk = blk.astype(BF16)
            for tin in range(t_ - lag):
                tout = tin + lag
                m_scr[tin * LANES:(tin + 1) * LANES, tout * LANES:(tout + 1) * LANES] = blk
        for tout in range(0, t_, 2):
            m_scr[(tout + 1) * LANES:(tout + 2) * LANES, tout * LANES:(tout + 1) * LANES] = zeros
        kdim = m_scr.shape[0]
        row_group = (lax.broadcasted_iota(jnp.int32, (kdim, LANES), 0) // (LANES // S5_SLAB_GROUPS)) % S5_SLAB_GROUPS
        lane_half = lax.broadcasted_iota(jnp.int32, (kdim, LANES), 1) // (LANES // 2)
        diff = row_group - lane_half
        for ri in range(2):
            wv = wc_ref[ri]
            vv = vc_ref[ri]
            for k in range(n_pair):
                sel = diff == 2 * k
                cols = slice(ri * half + k * LANES, ri * half + (k + 1) * LANES)
                w_scr[:, cols] = jnp.where(sel, wv, 0.0).astype(BF16)
                vt_scr[:, cols] = jnp.where(sel, vv, 0.0).astype(BF16)

    all_rows = rows * n_seq
    xf_scr[...] = u_ref[...].astype(F32)
    planes = [xf_scr[pl.ds(t, all_rows, stride=t_), :].astype(BF16) for t in range(t_)]
    u = jnp.concatenate(planes, axis=1)
    bc_scr[...] = jnp.dot(u, w_scr[...], preferred_element_type=F32)

    ar = a_ref[0:1, :]
    ai = a_ref[1:2, :]

    def tile_body(i, carry):
        new_carry = []
        for q in range(n_seq):
            sr, si = carry[q]
            base = pl.multiple_of(q * rows + i * 8, 8)
            tile = bc_scr[pl.ds(base, 8), :]
            rows_r, rows_i = [], []
            for r in range(8):
                rows_r.append(sr)
                rows_i.append(si)
                br = tile[r:r + 1, :half]
                bi = tile[r:r + 1, half:]
                sr, si = ar * sr - ai * si + br, ar * si + ai * sr + bi
            sin_scr[pl.ds(base, 8), :] = jnp.concatenate(
                [jnp.concatenate(rows_r, axis=0), jnp.concatenate(rows_i, axis=0)], axis=1)
            new_carry.append((sr, si))
        return tuple(new_carry)

    zero = jnp.zeros((1, half), F32)
    lax.fori_loop(0, rows // 8, tile_body, ((zero, zero),) * n_seq)

    s_in = sin_scr[...].astype(BF16)
    tile_w = 2 * LANES
    for jt in range(t_ // 2):
        k_hi = (2 * jt + 2) * LANES
        cols = slice(jt * tile_w, (jt + 1) * tile_w)
        y = (jnp.dot(u[:, :k_hi], m_scr[0:k_hi, cols], preferred_element_type=F32)
             + lax.dot_general(s_in, vt_scr[cols, :], nt, preferred_element_type=F32))
        y = _gelu_tanh(y)
        for tt in range(2):
            xf_scr[pl.ds(2 * jt + tt, all_rows, stride=t_), :] = y[:, tt * LANES:(tt + 1) * LANES]
    o_ref[...] = xf_scr[...].astype(o_ref.dtype)


def s5_prepare(log_dt, a_re, a_im, b_re, b_im, c_re, c_im, d_skip):
    g, p = a_re.shape
    c = b_re.shape[-1]
    t_ = S5_CHUNK
    gl = S5_SLAB_GROUPS
    ns = g // gl
    assert gl * c == LANES and 2 * p == LANES
    def twice(x):
        x = x.astype(F32)
        return jnp.concatenate([x, x], axis=-1)

    a_re, a_im = twice(a_re), twice(a_im)
    dt = jnp.exp(log_dt.astype(F32))[:, None]
    ks = jnp.arange(t_ + 1, dtype=F32)[:, None, None]
    mag = jnp.exp(ks * (dt * a_re)[None])
    ang = ks * (dt * a_im)[None]
    pr, pi = mag * jnp.cos(ang), mag * jnp.sin(ang)
    lb_re, lb_im = pr[1], pi[1]
    den = a_re * a_re + a_im * a_im
    f_re = ((lb_re - 1.0) * a_re + lb_im * a_im) / den
    f_im = (lb_im * a_re - (lb_re - 1.0) * a_im) / den
    br = twice(jnp.swapaxes(b_re, 1, 2))
    bi = twice(jnp.swapaxes(b_im, 1, 2))
    bb_re = f_re[:, None, :] * br - f_im[:, None, :] * bi
    bb_im = f_re[:, None, :] * bi + f_im[:, None, :] * br
    cr, ci = twice(c_re), twice(c_im)
    cp_re = cr[None] * pr[:, :, None, :] - ci[None] * pi[:, :, None, :]
    cp_im = cr[None] * pi[:, :, None, :] + ci[None] * pr[:, :, None, :]
    prr = pr[t_ - 1::-1][:t_]
    pir = pi[t_ - 1::-1][:t_]
    w_re = prr[:, :, None, :] * bb_re[None] - pir[:, :, None, :] * bb_im[None]
    w_im = prr[:, :, None, :] * bb_im[None] + pir[:, :, None, :] * bb_re[None]

    def rows_tgc(x):
        return x.reshape(t_, ns, gl * c, 2 * p).transpose(1, 0, 2, 3).reshape(ns, t_ * gl * c, 2 * p)

    first_copy = jnp.arange(2 * p) < p
    bbn = jnp.where(first_copy, bb_re, -bb_im).reshape(ns, gl * c, 2 * p)
    cl = rows_tgc(jnp.where(first_copy, cp_re[:t_], cp_im[:t_]))
    dsk = d_skip.astype(F32).reshape(ns, 1, gl * c)
    wc = jnp.stack([rows_tgc(w_re), rows_tgc(w_im)], axis=1)
    vc = jnp.stack([rows_tgc(cp_re[1:]), rows_tgc(-cp_im[1:])], axis=1)
    a_mat = jnp.stack([pr[t_][:, :p], pi[t_][:, :p]], axis=0).reshape(2, ns, gl * p).transpose(1, 0, 2)
    return bbn, cl, dsk, wc, vc, a_mat


def s5_mixer_gelu(proj, col_block0, n_batch, seq, mats):
    bbn, cl, dsk, wc, vc, a_mat = mats
    ns = bbn.shape[0]
    kdim = wc.shape[2]
    sdim = 2 * a_mat.shape[2]
    rows = seq // S5_CHUNK
    n_seq = 2 if n_batch % 2 == 0 else 1
    return pl.pallas_call(
        functools.partial(_s5_kernel, rows=rows, n_seq=n_seq),
        grid=(ns, n_batch // n_seq),
        in_specs=[pl.BlockSpec((n_seq * seq, LANES), lambda s, b: (b, col_block0 + s)),
                  pl.BlockSpec((None, LANES, LANES), lambda s, b: (s, 0, 0)),
                  pl.BlockSpec((None, kdim, LANES), lambda s, b: (s, 0, 0)),
                  pl.BlockSpec((None, 1, LANES), lambda s, b: (s, 0, 0)),
                  pl.BlockSpec((None, 2, kdim, LANES), lambda s, b: (s, 0, 0, 0)),
                  pl.BlockSpec((None, 2, kdim, LANES), lambda s, b: (s, 0, 0, 0)),
                  pl.BlockSpec((None, 2, sdim // 2), lambda s, b: (s, 0, 0))],
        out_specs=pl.BlockSpec((n_seq * seq, LANES), lambda s, b: (b, s)),
        out_shape=jax.ShapeDtypeStruct((n_batch * seq, ns * LANES), BF16),
        scratch_shapes=[pltpu.VMEM((n_seq * seq, LANES), F32),
                        pltpu.VMEM((n_seq * rows, sdim), F32),
                        pltpu.VMEM((n_seq * rows, sdim), F32),
                        pltpu.VMEM((kdim, kdim), BF16),
                        pltpu.VMEM((kdim, sdim), BF16),
                        pltpu.VMEM((kdim, sdim), BF16)],
        compiler_params=_params(2),
        name="s5_mixer",
    )(proj, bbn, cl, dsk, wc, vc, a_mat)


def _attn_kernel(q_ref, k_ref, v_ref, o_ref, acc_scr, m_scr, l_scr, *, seq):
    step = pl.program_id(2)
    n_g = len(DILATION_PATTERNS)
    blk = ATTN_BLOCK
    row = lax.broadcasted_iota(jnp.int32, (blk, blk), 0)
    col = lax.broadcasted_iota(jnp.int32, (blk, blk), 1)
    cur_ok = col <= row
    prev_ok = col >= row
    nt = (((1,), (1,)), ((), ()))
    ones = jnp.ones((blk, HEAD_DIM), BF16)

    def run_group(step_i, dil):
        first = step_i == 0
        last = step_i == n_g - 1
        n_qb = seq // dil // blk

        def load(ref, start):
            if dil == 1:
                return ref[pl.ds(start, blk), :]
            return ref[pl.ds(start, blk, stride=dil), :]

        def store(ref, start, val):
            if dil == 1:
                ref[pl.ds(start, blk), :] = val
            else:
                ref[pl.ds(start, blk, stride=dil), :] = val

        def block(base, prev, prev_cond):
            q = load(q_ref, base).astype(BF16)
            kc = load(k_ref, base).astype(BF16)
            vc = jnp.concatenate([load(v_ref, base).astype(BF16), ones], axis=1)
            s_c = jnp.where(cur_ok, lax.dot_general(q, kc, nt, preferred_element_type=F32), NEG_BIG)
            m_blk = jnp.max(s_c, axis=1, keepdims=True)
            if prev is not None:
                ok = prev_ok if prev_cond is None else jnp.logical_and(prev_ok, prev_cond)
                s_p = jnp.where(ok, lax.dot_general(q, prev[0], nt, preferred_element_type=F32), NEG_BIG)
                m_blk = jnp.maximum(m_blk, jnp.max(s_p, axis=1, keepdims=True))
            if first:
                m_new = m_blk
            else:
                m_old = load(m_scr, base)
                m_new = jnp.maximum(m_old, m_blk)
            pv = jnp.dot(jnp.exp(s_c - m_new).astype(BF16), vc, preferred_element_type=F32)
            if prev is not None:
                pv = pv + jnp.dot(jnp.exp(s_p - m_new).astype(BF16), prev[1], preferred_element_type=F32)
            acc_new = pv[:, :HEAD_DIM]
            l_new = pv[:, HEAD_DIM:]
            if not first:
                alpha = jnp.exp(m_old - m_new)
                acc_new = alpha * load(acc_scr, base) + acc_new
                l_new = alpha * load(l_scr, base) + l_new
            if last and dil == 1:
                o_ref[pl.ds(pl.multiple_of(base, blk), blk), :] = (acc_new / l_new).astype(o_ref.dtype)
            else:
                store(acc_scr, base, acc_new)
                store(l_scr, base, l_new)
                if not last:
                    store(m_scr, base, jnp.broadcast_to(m_new, (blk, blk)))
            return kc, vc

        if n_qb >= ATTN_UNROLL:
            assert n_qb % ATTN_UNROLL == 0

            def res_body(res, _):
                def body(it, carry):
                    prev = carry
                    for u in range(ATTN_UNROLL):
                        base = (it * ATTN_UNROLL + u) * (blk * dil) + res
                        prev = block(base, prev, (it > 0) if u == 0 else None)
                    return prev
                init = (jnp.zeros((blk, HEAD_DIM), BF16), jnp.zeros((blk, 2 * HEAD_DIM), BF16))
                lax.fori_loop(0, n_qb // ATTN_UNROLL, body, init)
                return 0
            if dil == 1:
                res_body(0, 0)
            else:
                lax.fori_loop(0, dil, res_body, 0)
        else:
            assert ATTN_UNROLL % n_qb == 0
            res_per_body = ATTN_UNROLL // n_qb
            assert dil % res_per_body == 0

            def body(it, _):
                for rr in range(res_per_body):
                    res = it * res_per_body + rr
                    prev = None
                    for u in range(n_qb):
                        prev = block(u * (blk * dil) + res, prev, None)
                return 0
            lax.fori_loop(0, dil // res_per_body, body, 0)

    for step_i, gi in enumerate(ATTN_ORDER):
        window, dil = DILATION_PATTERNS[gi]
        assert window // dil == blk
        pl.when(step == step_i)(functools.partial(run_group, step_i, dil))

    if DILATION_PATTERNS[ATTN_ORDER[-1]][1] != 1:
        @pl.when(step == n_g - 1)
        def _():
            o_ref[...] = (acc_scr[...] / l_scr[...]).astype(o_ref.dtype)


def dilated_attention(qk, v, n_batch, seq, n_heads, side=()):
    n_g = len(DILATION_PATTERNS)
    width = n_g * n_heads

    def group_of_step(s):
        gi = ATTN_ORDER[-1]
        for step_i in range(n_g - 2, -1, -1):
            gi = jnp.where(s == step_i, ATTN_ORDER[step_i], gi)
        return gi

    def col_spec(part):
        return pl.BlockSpec((seq, LANES), lambda b, h, s: (b, part * width + group_of_step(s) * n_heads + h))

    grid = (n_batch, n_heads, n_g)
    side_specs, side_shapes = _side_cast_specs(side, grid)
    return pl.pallas_call(
        _with_side_casts(functools.partial(_attn_kernel, seq=seq), 3, 1, len(side)),
        grid=grid,
        in_specs=[col_spec(0), col_spec(1), col_spec(0)] + side_specs,
        out_specs=[pl.BlockSpec((seq, LANES), lambda b, h, g: (b, h))] + side_specs,
        out_shape=[jax.ShapeDtypeStruct((n_batch * seq, n_heads * LANES), BF16)] + side_shapes,
        scratch_shapes=[pltpu.VMEM((seq, LANES), F32)] * 3,
        compiler_params=_params(3),
        name="dilated_attention",
    )(qk, qk, v, *[src for src, _ in side])


def _sigmoid(x):
    return 1.0 / (1.0 + jnp.exp(-x))


def _mix_kernel(y_ref, at_ref, wa_ref, wb_ref, wo_ref, gs_ref, ga_ref, o_ref):
    y = y_ref[...]
    attn = jnp.dot(at_ref[...], wo_ref[...], preferred_element_type=F32)
    attn_gated = _sigmoid(ga_ref[...].astype(F32)) * attn
    glu_b = jnp.dot(y, wb_ref[...], preferred_element_type=F32)
    ssm_gate = _sigmoid(gs_ref[...].astype(F32)) * _sigmoid(glu_b)
    glu_a = jnp.dot(y, wa_ref[...], preferred_element_type=F32)
    o_ref[...] = (ssm_gate * glu_a + attn_gated).astype(o_ref.dtype)


def gated_merge(y, attn, w_glu, w_ao, proj, col_gs, col_ga, tm, tn, side=()):
    m, ky = y.shape
    ka = attn.shape[1]
    d = w_ao.shape[1]
    nj = d // tn
    grid = (m // tm, nj)
    side_specs, side_shapes = _side_cast_specs(side, grid)
    return pl.pallas_call(
        _with_side_casts(_mix_kernel, 7, 1, len(side)),
        grid=grid,
        in_specs=[pl.BlockSpec((tm, ky), lambda i, j: (i, 0)),
                  pl.BlockSpec((tm, ka), lambda i, j: (i, 0)),
                  pl.BlockSpec((ky, tn), lambda i, j: (0, j)),
                  pl.BlockSpec((ky, tn), lambda i, j: (0, nj + j)),
                  pl.BlockSpec((ka, tn), lambda i, j: (0, j)),
                  pl.BlockSpec((tm, tn), lambda i, j: (i, col_gs // tn + j)),
                  pl.BlockSpec((tm, tn), lambda i, j: (i, col_ga // tn + j))] + side_specs,
        out_specs=[pl.BlockSpec((tm, tn), lambda i, j: (i, j))] + side_specs,
        out_shape=[jax.ShapeDtypeStruct((m, d), BF16)] + side_shapes,
        compiler_params=_params(2),
        name="gated_merge",
    )(y, attn, w_glu, w_glu, w_ao, proj, proj, *[src for src, _ in side])


def _ffn_up_kernel(h_ref, wa_ref, wg_ref, ca_ref, cg_ref, o_ref, ua_scr, ug_scr, halo_a, halo_g,
                   *, tm, blocks_per_seq):
    i = pl.program_id(0)
    j = pl.program_id(1)
    h = h_ref[...]
    seq_start = (i % blocks_per_seq) == 0

    def conv(w_ref, c_ref, u_scr, halo):
        u_scr[pl.ds(HALO_ROWS, tm), :] = jnp.dot(h, w_ref[...].astype(BF16), preferred_element_type=F32)
        u_scr[pl.ds(0, HALO_ROWS), :] = jnp.where(seq_start, 0.0, halo[j])
        halo[j] = u_scr[pl.ds(tm, HALO_ROWS), :]
        cw = c_ref[...]
        return (cw[2:3, :] * u_scr[pl.ds(HALO_ROWS, tm), :]
                + cw[1:2, :] * u_scr[pl.ds(HALO_ROWS - 1, tm), :]
                + cw[0:1, :] * u_scr[pl.ds(HALO_ROWS - 2, tm), :]
                + cw[3:4, :])

    a = conv(wa_ref, ca_ref, ua_scr, halo_a)
    silu_a = a * _sigmoid(a)
    gv = conv(wg_ref, cg_ref, ug_scr, halo_g)
    o_ref[...] = (silu_a * gv).astype(o_ref.dtype)


def ffn_up(h, w_up, conv_p, seq, tm, tn):
    m, k = h.shape
    f = w_up.shape[1] // 2
    nj = -(-f // tn)
    assert f % LANES == 0 and nj >= 2

    def col0(j, half=0):
        return pl.multiple_of(half * f + jnp.minimum(j * tn, f - tn), LANES)

    lane_tile = pl.Element(tn)
    u_scratch = pltpu.VMEM((tm + HALO_ROWS, tn), F32)
    halo_scratch = pltpu.VMEM((nj, HALO_ROWS, tn), F32)
    return pl.pallas_call(
        functools.partial(_ffn_up_kernel, tm=tm, blocks_per_seq=seq // tm),
        grid=(m // tm, nj),
        in_specs=[pl.BlockSpec((tm, k), lambda i, j: (i, 0)),
                  pl.BlockSpec((pl.Element(k), lane_tile), lambda i, j: (0, col0(j))),
                  pl.BlockSpec((pl.Element(k), lane_tile), lambda i, j: (0, col0(j, 1))),
                  pl.BlockSpec((pl.Element(HALO_ROWS), lane_tile), lambda i, j: (0, col0(j))),
                  pl.BlockSpec((pl.Element(HALO_ROWS), lane_tile), lambda i, j: (0, col0(j, 1)))],
        out_specs=pl.BlockSpec((tm, tn), lambda i, j: (i, j)),
        out_shape=jax.ShapeDtypeStruct((m, nj * tn), BF16),
        scratch_shapes=[u_scratch, u_scratch, halo_scratch, halo_scratch],
        compiler_params=_params(2),
        name="ffn_up",
    )(h, w_up, w_up, conv_p, conv_p)


def _ffn_down_kernel(a1_ref, a2_ref, w1_ref, w2_ref, r_ref, o_ref):
    o_ref[...] = (r_ref[...]
                  + jnp.dot(a1_ref[...], w1_ref[...], preferred_element_type=F32)
                  + jnp.dot(a2_ref[...], w2_ref[...], preferred_element_type=F32))


def ffn_down(act, up_tile, w, res, tm, tn):
    m, width = act.shape
    k, n = w.shape
    k1 = width - up_tile
    k2 = k - k1
    assert 0 < k2 <= up_tile and width % k2 == 0 and k1 % k2 == 0
    return pl.pallas_call(
        _ffn_down_kernel,
        grid=(m // tm, n // tn),
        in_specs=[pl.BlockSpec((tm, k1), lambda i, j: (i, 0)),
                  pl.BlockSpec((tm, k2), lambda i, j: (i, width // k2 - 1)),
                  pl.BlockSpec((k1, tn), lambda i, j: (0, j)),
                  pl.BlockSpec((k2, tn), lambda i, j: (k1 // k2, j)),
                  pl.BlockSpec((tm, tn), lambda i, j: (i, j))],
        out_specs=pl.BlockSpec((tm, tn), lambda i, j: (i, j)),
        out_shape=jax.ShapeDtypeStruct((m, n), F32),
        compiler_params=_params(2),
        name="ffn_down",
    )(act, act, w, w, res)


def kernel(x, g_mix, w_in, ssm_log_dt, ssm_a_re, ssm_a_im, ssm_b_re, ssm_b_im, ssm_c_re, ssm_c_im,
           ssm_d, w_glu, w_attn_out, w_out, g_ffn, w_up, conv_w, conv_b, w_down, g_final):
    b, l, d = x.shape
    depth = g_mix.shape[0]
    ssm_width = ssm_a_re.shape[1] * ssm_b_re.shape[-1]
    attn_width = w_attn_out.shape[1]
    n_heads = attn_width // HEAD_DIM
    qkv_width = len(DILATION_PATTERNS) * attn_width
    d_ff = w_down.shape[1]
    ff_tile = 512
    proj_tile = 1024
    merge_tile = 512
    row_tile = 1024
    n_row_tiles = b * l // row_tile
    ssm_tiles = ssm_width // proj_tile
    qkv_tiles = 3 * qkv_width // proj_tile
    gate_tiles = 2 * d // proj_tile

    xf = x.reshape(b * l, d).astype(F32)
    for i in range(depth):
        h = rmsnorm(xf, g_mix[i], BF16)
        w_in_b = w_in[i].astype(BF16)
        n_in_tiles = ssm_tiles + gate_tiles
        proj, w_up_b = matmul(h, w_in_b, BF16, row_tile, proj_tile, n_in_tiles,
                              lambda j: jnp.where(j < ssm_tiles, j, j + qkv_tiles), "in_proj",
                              side=[_cast_chunks(w_up[i], n_row_tiles * n_in_tiles)])
        (qk,) = qk_projection(h, w_in_b, ssm_tiles, qkv_width, l, row_tile, proj_tile)
        v_tile0 = ssm_tiles + 2 * qkv_width // proj_tile
        (v,) = matmul(h, w_in_b, F32, row_tile, proj_tile, qkv_width // proj_tile,
                      lambda j: v_tile0 + j, "v_proj")
        mats = s5_prepare(ssm_log_dt[i], ssm_a_re[i], ssm_a_im[i], ssm_b_re[i], ssm_b_im[i],
                          ssm_c_re[i], ssm_c_im[i], ssm_d[i])
        y = s5_mixer_gelu(proj, 0, b, l, mats)
        n_attn_steps = b * n_heads * len(DILATION_PATTERNS)
        attn, w_glu_b, w_ao_b, w_out_b = dilated_attention(
            qk, v, b, l, n_heads,
            side=[_cast_chunks(w_glu[i], n_attn_steps), _cast_chunks(w_attn_out[i], n_attn_steps),
                  _cast_chunks(w_out[i], n_attn_steps)])
        mixed, w_down_b = gated_merge(y, attn, w_glu_b, w_ao_b, proj, ssm_width, ssm_width + d,
                                      row_tile, merge_tile,
                                      side=[_cast_chunks(w_down[i], n_row_tiles * (d // merge_tile))])
        xf = matmul_residual(mixed, w_out_b, xf, row_tile, 512, 1, "out_proj")

        h = rmsnorm(xf, g_ffn[i], BF16)
        zrow = jnp.zeros((HALO_ROWS - CONV_WIDTH - 1, 2 * d_ff), F32)
        conv_p = jnp.concatenate([conv_w[i].astype(F32), conv_b[i][None].astype(F32), zrow], axis=0)
        act = ffn_up(h, w_up_b, conv_p, l, row_tile, ff_tile)
        xf = ffn_down(act, ff_tile, w_down_b, xf, 512, 512)
    out = rmsnorm(xf, g_final, x.dtype)
    return out.reshape(b, l, d)
```

```python
import functools
import math

import jax
import jax.numpy as jnp
from jax import lax
from jax.experimental import pallas as pl
from jax.experimental.pallas import tpu as pltpu

F32 = jnp.float32
BF16 = jnp.bfloat16

RMS_EPS = 1e-5
LANES = 128
HEAD_DIM = 128
ROPE_DIM = HEAD_DIM // 4
ROPE_THETA = 500000.0
DILATION_PATTERNS = ((128, 1), (512, 4), (2048, 16))
ATTN_BLOCK = 128
ATTN_UNROLL = 32
ATTN_ORDER = tuple(sorted(range(len(DILATION_PATTERNS)), key=lambda i: -DILATION_PATTERNS[i][1]))
S5_CHUNK = 16
S5_SLAB_GROUPS = 8
CONV_WIDTH = 3
HALO_ROWS = 8
NEG_BIG = -1e30
VMEM_LIMIT = 56 * 1024 * 1024


def _params(n_axes):
    return pltpu.CompilerParams(dimension_semantics=("arbitrary",) * n_axes,
                                vmem_limit_bytes=VMEM_LIMIT)


def _rmsnorm_kernel(x_ref, g_ref, o_ref):
    x = x_ref[...]
    y = x * lax.rsqrt(jnp.mean(x * x, axis=-1, keepdims=True) + RMS_EPS)
    o_ref[...] = (y * g_ref[...]).astype(o_ref.dtype)


def rmsnorm(x2d, g, out_dtype, tm=512):
    m, d = x2d.shape
    return pl.pallas_call(
        _rmsnorm_kernel,
        grid=(m // tm,),
        in_specs=[pl.BlockSpec((tm, d), lambda i: (i, 0)),
                  pl.BlockSpec((1, d), lambda i: (0, 0))],
        out_specs=pl.BlockSpec((tm, d), lambda i: (i, 0)),
        out_shape=jax.ShapeDtypeStruct((m, d), out_dtype),
        compiler_params=_params(1),
        name="rmsnorm",
    )(x2d, g.reshape(1, d).astype(F32))


def _with_side_casts(body, n_in, n_out, n_side):
    def kernel(*refs):
        ins = refs[:n_in]
        srcs = refs[n_in:n_in + n_side]
        outs = refs[n_in + n_side:n_in + n_side + n_out]
        dsts = refs[n_in + n_side + n_out:n_in + 2 * n_side + n_out]
        rest = refs[n_in + 2 * n_side + n_out:]
        body(*ins, *outs, *rest)
        for src, dst in zip(srcs, dsts):
            dst[...] = src[...].astype(dst.dtype)
    return kernel


def _cast_chunks(w, n_steps):
    bf16_rows = 16
    rows = next(r for r in range(bf16_rows, w.shape[0] + 1, bf16_rows)
                if w.shape[0] % r == 0 and w.shape[0] // r <= n_steps)
    return w, rows


def _side_cast_specs(side, grid):
    n_steps = math.prod(grid)

    def linear_step(*idx):
        s = idx[0]
        for extent, i in zip(grid[1:], idx[1:]):
            s = s * extent + i
        return s

    specs, shapes = [], []
    for src, rows in side:
        n_chunks = src.shape[0] // rows
        assert n_chunks * rows == src.shape[0] and n_chunks <= n_steps
        specs.append(pl.BlockSpec((rows, src.shape[1]),
                                  lambda *idx, n_chunks=n_chunks: (jnp.minimum(linear_step(*idx), n_chunks - 1), 0)))
        shapes.append(jax.ShapeDtypeStruct(src.shape, BF16))
    return specs, shapes


def _mm_kernel(a_ref, w_ref, o_ref):
    o_ref[...] = jnp.dot(a_ref[...], w_ref[...], preferred_element_type=F32).astype(o_ref.dtype)


def matmul(a, w, out_dtype, tm, tn, n_col_tiles, w_col_tile, name, side=()):
    m, k = a.shape
    grid = (m // tm, n_col_tiles)
    side_specs, side_shapes = _side_cast_specs(side, grid)
    return pl.pallas_call(
        _with_side_casts(_mm_kernel, 2, 1, len(side)),
        grid=grid,
        in_specs=[pl.BlockSpec((tm, k), lambda i, j: (i, 0)),
                  pl.BlockSpec((k, tn), lambda i, j: (0, w_col_tile(j)))] + side_specs,
        out_specs=[pl.BlockSpec((tm, tn), lambda i, j: (i, j))] + side_specs,
        out_shape=[jax.ShapeDtypeStruct((m, n_col_tiles * tn), out_dtype)] + side_shapes,
        compiler_params=_params(2),
        name=name,
    )(a, w, *[src for src, _ in side])


def _mm_res_kernel(a_ref, w_ref, r_ref, o_ref):
    k = pl.program_id(2)
    part = jnp.dot(a_ref[...], w_ref[...], preferred_element_type=F32)

    @pl.when(k == 0)
    def _():
        o_ref[...] = r_ref[...] + part

    @pl.when(k != 0)
    def _():
        o_ref[...] += part


def matmul_residual(a, w, res, tm, tn, nk, name, a_single_buffer=False):
    m = a.shape[0]
    k, n = w.shape
    tk = k // nk
    a_mode = pl.Buffered(1) if a_single_buffer else None
    return pl.pallas_call(
        _mm_res_kernel,
        grid=(m // tm, n // tn, nk),
        in_specs=[pl.BlockSpec((tm, tk), lambda i, j, kk: (i, kk), pipeline_mode=a_mode),
                  pl.BlockSpec((tk, tn), lambda i, j, kk: (kk, j)),
                  pl.BlockSpec((tm, tn), lambda i, j, kk: (i, j))],
        out_specs=pl.BlockSpec((tm, tn), lambda i, j, kk: (i, j)),
        out_shape=jax.ShapeDtypeStruct((m, n), F32),
        compiler_params=_params(3),
        name=name,
    )(a, w, res)


def _mm_res_norm_kernel(a_ref, w_ref, r_ref, g_ref, o_ref, h_ref, row_scr):
    j = pl.program_id(1)
    n_tiles, _, tn = row_scr.shape
    x = r_ref[...] + jnp.dot(a_ref[...], w_ref[...], preferred_element_type=F32)
    o_ref[...] = x
    row_scr[j] = x

    @pl.when(j == n_tiles - 1)
    def _():
        ssq = jnp.sum(row_scr[0] * row_scr[0], axis=-1, keepdims=True)
        for t in range(1, n_tiles):
            xt = row_scr[t]
            ssq = ssq + jnp.sum(xt * xt, axis=-1, keepdims=True)
        scale = lax.rsqrt(ssq / (n_tiles * tn) + RMS_EPS)
        for t in range(n_tiles):
            cols = slice(t * tn, (t + 1) * tn)
            h_ref[:, cols] = (row_scr[t] * scale * g_ref[:, cols]).astype(h_ref.dtype)


def matmul_residual_norm(a, w, res, g, tm, tn, name):
    m, k = a.shape
    n = w.shape[1]
    return pl.pallas_call(
        _mm_res_norm_kernel,
        grid=(m // tm, n // tn),
        in_specs=[pl.BlockSpec((tm, k), lambda i, j: (i, 0)),
                  pl.BlockSpec((k, tn), lambda i, j: (0, j)),
                  pl.BlockSpec((tm, tn), lambda i, j: (i, j)),
                  pl.BlockSpec((1, n), lambda i, j: (0, 0))],
        out_specs=[pl.BlockSpec((tm, tn), lambda i, j: (i, j)),
                   pl.BlockSpec((tm, n), lambda i, j: (i, 0))],
        out_shape=[jax.ShapeDtypeStruct((m, n), F32), jax.ShapeDtypeStruct((m, n), BF16)],
        scratch_shapes=[pltpu.VMEM((n // tn, tm, tn), F32)],
        compiler_params=_params(2),
        name=name,
    )(a, w, res, g.reshape(1, n).astype(F32))


def rope_tables(seq):
    half = ROPE_DIM // 2
    inv_freq = ROPE_THETA ** (-jnp.arange(0, ROPE_DIM, 2, dtype=F32) / ROPE_DIM)
    ang = jnp.arange(seq, dtype=F32)[:, None] * inv_freq[None, :]
    cos, sin = jnp.cos(ang), jnp.sin(ang)
    pad = HEAD_DIM - ROPE_DIM
    cos_t = jnp.concatenate([cos, cos, jnp.ones((seq, pad), F32)], axis=1)
    zeros_h = jnp.zeros((seq, half), F32)
    zeros_p = jnp.zeros((seq, pad), F32)
    sin_lo = jnp.concatenate([-sin, zeros_h, zeros_p], axis=1)
    sin_hi = jnp.concatenate([zeros_h, sin, zeros_p], axis=1)
    return cos_t, sin_lo, sin_hi


def _rotary(x, cos_t, sin_lo, sin_hi):
    half = ROPE_DIM // 2
    up = pltpu.roll(x, HEAD_DIM - half, axis=1)
    dn = pltpu.roll(x, half, axis=1)
    return x * cos_t + up * sin_lo + dn * sin_hi


def _qkv_proj_kernel(h_ref, w_ref, cos_ref, slo_ref, shi_ref, o_ref, *, heads_per_tile):
    acc = jnp.dot(h_ref[...], w_ref[...], preferred_element_type=F32)
    cos_t, sin_lo, sin_hi = cos_ref[...], slo_ref[...], shi_ref[...]
    for hh in range(heads_per_tile):
        sl = slice(hh * HEAD_DIM, (hh + 1) * HEAD_DIM)
        o_ref[:, sl] = _rotary(acc[:, sl], cos_t, sin_lo, sin_hi)


def qk_projection(h, w, col_tile0, qkv_width, seq, tm, tn, side=()):
    m, k = h.shape
    n_q_tiles = qkv_width // tn
    cos_t, sin_lo, sin_hi = rope_tables(seq)
    scale = HEAD_DIM ** -0.5
    tables = (jnp.stack([cos_t * scale, cos_t]),
              jnp.stack([sin_lo * scale, sin_lo]),
              jnp.stack([sin_hi * scale, sin_hi]))

    def tab_index(i, j):
        return (jnp.where(j < n_q_tiles, 0, 1), i % (seq // tm), 0)

    tab_spec = pl.BlockSpec((None, tm, HEAD_DIM), tab_index)
    grid = (m // tm, 2 * n_q_tiles)
    side_specs, side_shapes = _side_cast_specs(side, grid)
    body = functools.partial(_qkv_proj_kernel, heads_per_tile=tn // HEAD_DIM)
    return pl.pallas_call(
        _with_side_casts(body, 5, 1, len(side)),
        grid=grid,
        in_specs=[pl.BlockSpec((tm, k), lambda i, j: (i, 0)),
                  pl.BlockSpec((k, tn), lambda i, j: (0, col_tile0 + j)),
                  tab_spec, tab_spec, tab_spec] + side_specs,
        out_specs=[pl.BlockSpec((tm, tn), lambda i, j: (i, j))] + side_specs,
        out_shape=[jax.ShapeDtypeStruct((m, 2 * qkv_width), F32)] + side_shapes,
        compiler_params=_params(2),
        name="qk_proj",
    )(h, w, *tables, *[src for src, _ in side])


def _gelu_tanh(x):
    c = math.sqrt(2.0 / math.pi)
    return 0.5 * x * (1.0 + jnp.tanh(c * (x + 0.044715 * (x * x * x))))


def _split_bf16(x):
    hi = x.astype(BF16)
    return hi, (x - hi.astype(F32)).astype(BF16)


def _s5_kernel(u_ref, bbn_ref, cl_ref, dsk_ref, wc_ref, vc_ref, a_ref, o_ref,
               xf_scr, bc_scr, sin_scr, m_scr, w_scr, vt_scr, *, rows, n_seq):
    t_ = S5_CHUNK
    half = bc_scr.shape[1] // 2
    n_pair = half // LANES
    nt = (((1,), (1,)), ((), ()))

    @pl.when(pl.program_id(1) == 0)
    def _():
        b_hi, b_lo = _split_bf16(bbn_ref[...])
        c_hi, c_lo = _split_bf16(cl_ref[...])
        kt = (lax.dot_general(b_hi, c_hi, nt, preferred_element_type=F32)
              + lax.dot_general(b_hi, c_lo, nt, preferred_element_type=F32)
              + lax.dot_general(b_lo, c_hi, nt, preferred_element_type=F32))
        brow = lax.broadcasted_iota(jnp.int32, (LANES, LANES), 0)
        bcol = lax.broadcasted_iota(jnp.int32, (LANES, LANES), 1)
        chan = LANES // S5_SLAB_GROUPS
        same_group = (brow // chan) == (bcol // chan)
        zeros = jnp.zeros((LANES, LANES), BF16)
        for lag in range(t_):
            blk = jnp.where(same_group, kt[:, lag * LANES:(lag + 1) * LANES], 0.0)
            if lag == 0:
                blk = blk + jnp.where(brow == bcol, dsk_ref[...], 0.0)
            blk = blk.astype(BF16)
            for tin in range(t_ - lag):
                tout = tin + lag
                m_scr[tin * LANES:(tin + 1) * LANES, tout * LANES:(tout + 1) * LANES] = blk
        for tout in range(0, t_, 2):
            m_scr[(tout + 1) * LANES:(tout + 2) * LANES, tout * LANES:(tout + 1) * LANES] = zeros
        kdim = m_scr.shape[0]
        row_group = (lax.broadcasted_iota(jnp.int32, (kdim, LANES), 0) // (LANES // S5_SLAB_GROUPS)) % S5_SLAB_GROUPS
        lane_half = lax.broadcasted_iota(jnp.int32, (kdim, LANES), 1) // (LANES // 2)
        diff = row_group - lane_half
        for ri in range(2):
            wv = wc_ref[ri]
            vv = vc_ref[ri]
            for k in range(n_pair):
                sel = diff == 2 * k
                cols = slice(ri * half + k * LANES, ri * half + (k + 1) * LANES)
                w_scr[:, cols] = jnp.where(sel, wv, 0.0).astype(BF16)
                vt_scr[:, cols] = jnp.where(sel, vv, 0.0).astype(BF16)

    all_rows = rows * n_seq
    xf_scr[...] = u_ref[...].astype(F32)
    planes = [xf_scr[pl.ds(t, all_rows, stride=t_), :].astype(BF16) for t in range(t_)]
    u = jnp.concatenate(planes, axis=1)
    bc_scr[...] = jnp.dot(u, w_scr[...], preferred_element_type=F32)

    ar = a_ref[0:1, :]
    ai = a_ref[1:2, :]

    def tile_body(i, carry):
        new_carry = []
        for q in range(n_seq):
            sr, si = carry[q]
            base = pl.multiple_of(q * rows + i * 8, 8)
            tile = bc_scr[pl.ds(base, 8), :]
            rows_r, rows_i = [], []
            for r in range(8):
                rows_r.append(sr)
                rows_i.append(si)
                br = tile[r:r + 1, :half]
                bi = tile[r:r + 1, half:]
                sr, si = ar * sr - ai * si + br, ar * si + ai * sr + bi
            sin_scr[pl.ds(base, 8), :] = jnp.concatenate(
                [jnp.concatenate(rows_r, axis=0), jnp.concatenate(rows_i, axis=0)], axis=1)
            new_carry.append((sr, si))
        return tuple(new_carry)

    zero = jnp.zeros((1, half), F32)
    lax.fori_loop(0, rows // 8, tile_body, ((zero, zero),) * n_seq)

    s_in = sin_scr[...].astype(BF16)
    tile_w = 2 * LANES
    for jt in range(t_ // 2):
        k_hi = (2 * jt + 2) * LANES
        cols = slice(jt * tile_w, (jt + 1) * tile_w)
        y = (jnp.dot(u[:, :k_hi], m_scr[0:k_hi, cols], preferred_element_type=F32)
             + lax.dot_general(s_in, vt_scr[cols, :], nt, preferred_element_type=F32))
        y = _gelu_tanh(y)
        for tt in range(2):
            xf_scr[pl.ds(2 * jt + tt, all_rows, stride=t_), :] = y[:, tt * LANES:(tt + 1) * LANES]
    o_ref[...] = xf_scr[...].astype(o_ref.dtype)


def s5_prepare(log_dt, a_re, a_im, b_re, b_im, c_re, c_im, d_skip):
    g, p = a_re.shape
    c = b_re.shape[-1]
    t_ = S5_CHUNK
    gl = S5_SLAB_GROUPS
    ns = g // gl
    assert gl * c == LANES and 2 * p == LANES
    def twice(x):
        x = x.astype(F32)
        return jnp.concatenate([x, x], axis=-1)

    a_re, a_im = twice(a_re), twice(a_im)
    dt = jnp.exp(log_dt.astype(F32))[:, None]
    ks = jnp.arange(t_ + 1, dtype=F32)[:, None, None]
    mag = jnp.exp(ks * (dt * a_re)[None])
    ang = ks * (dt * a_im)[None]
    pr, pi = mag * jnp.cos(ang), mag * jnp.sin(ang)
    lb_re, lb_im = pr[1], pi[1]
    den = a_re * a_re + a_im * a_im
    f_re = ((lb_re - 1.0) * a_re + lb_im * a_im) / den
    f_im = (lb_im * a_re - (lb_re - 1.0) * a_im) / den
    br = twice(jnp.swapaxes(b_re, 1, 2))
    bi = twice(jnp.swapaxes(b_im, 1, 2))
    bb_re = f_re[:, None, :] * br - f_im[:, None, :] * bi
    bb_im = f_re[:, None, :] * bi + f_im[:, None, :] * br
    cr, ci = twice(c_re), twice(c_im)
    cp_re = cr[None] * pr[:, :, None, :] - ci[None] * pi[:, :, None, :]
    cp_im = cr[None] * pi[:, :, None, :] + ci[None] * pr[:, :, None, :]
    prr = pr[t_ - 1::-1][:t_]
    pir = pi[t_ - 1::-1][:t_]
    w_re = prr[:, :, None, :] * bb_re[None] - pir[:, :, None, :] * bb_im[None]
    w_im = prr[:, :, None, :] * bb_im[None] + pir[:, :, None, :] * bb_re[None]

    def rows_tgc(x):
        return x.reshape(t_, ns, gl * c, 2 * p).transpose(1, 0, 2, 3).reshape(ns, t_ * gl * c, 2 * p)

    first_copy = jnp.arange(2 * p) < p
    bbn = jnp.where(first_copy, bb_re, -bb_im).reshape(ns, gl * c, 2 * p)
    cl = rows_tgc(jnp.where(first_copy, cp_re[:t_], cp_im[:t_]))
    dsk = d_skip.astype(F32).reshape(ns, 1, gl * c)
    wc = jnp.stack([rows_tgc(w_re), rows_tgc(w_im)], axis=1)
    vc = jnp.stack([rows_tgc(cp_re[1:]), rows_tgc(-cp_im[1:])], axis=1)
    a_mat = jnp.stack([pr[t_][:, :p], pi[t_][:, :p]], axis=0).reshape(2, ns, gl * p).transpose(1, 0, 2)
    return bbn, cl, dsk, wc, vc, a_mat


def s5_mixer_gelu(proj, col_block0, n_batch, seq, mats):
    bbn, cl, dsk, wc, vc, a_mat = mats
    ns = bbn.shape[0]
    kdim = wc.shape[2]
    sdim = 2 * a_mat.shape[2]
    rows = seq // S5_CHUNK
    n_seq = 2 if n_batch % 2 == 0 else 1
    return pl.pallas_call(
        functools.partial(_s5_kernel, rows=rows, n_seq=n_seq),
        grid=(ns, n_batch // n_seq),
        in_specs=[pl.BlockSpec((n_seq * seq, LANES), lambda s, b: (b, col_block0 + s)),
                  pl.BlockSpec((None, LANES, LANES), lambda s, b: (s, 0, 0)),
                  pl.BlockSpec((None, kdim, LANES), lambda s, b: (s, 0, 0)),
                  pl.BlockSpec((None, 1, LANES), lambda s, b: (s, 0, 0)),
                  pl.BlockSpec((None, 2, kdim, LANES), lambda s, b: (s, 0, 0, 0)),
                  pl.BlockSpec((None, 2, kdim, LANES), lambda s, b: (s, 0, 0, 0)),
                  pl.BlockSpec((None, 2, sdim // 2), lambda s, b: (s, 0, 0))],
        out_specs=pl.BlockSpec((n_seq * seq, LANES), lambda s, b: (b, s)),
        out_shape=jax.ShapeDtypeStruct((n_batch * seq, ns * LANES), BF16),
        scratch_shapes=[pltpu.VMEM((n_seq * seq, LANES), F32),
                        pltpu.VMEM((n_seq * rows, sdim), F32),
                        pltpu.VMEM((n_seq * rows, sdim), F32),
                        pltpu.VMEM((kdim, kdim), BF16),
                        pltpu.VMEM((kdim, sdim), BF16),
                        pltpu.VMEM((kdim, sdim), BF16)],
        compiler_params=_params(2),
        name="s5_mixer",
    )(proj, bbn, cl, dsk, wc, vc, a_mat)


def _attn_kernel(q_ref, k_ref, v_ref, o_ref, acc_scr, m_scr, l_scr, *, seq):
    step = pl.program_id(2)
    n_g = len(DILATION_PATTERNS)
    blk = ATTN_BLOCK
    row = lax.broadcasted_iota(jnp.int32, (blk, blk), 0)
    col = lax.broadcasted_iota(jnp.int32, (blk, blk), 1)
    cur_ok = col <= row
    prev_ok = col >= row
    nt = (((1,), (1,)), ((), ()))
    ones = jnp.ones((blk, HEAD_DIM), BF16)

    def run_group(step_i, dil):
        first = step_i == 0
        last = step_i == n_g - 1
        n_qb = seq // dil // blk

        def load(ref, start):
            if dil == 1:
                return ref[pl.ds(start, blk), :]
            return ref[pl.ds(start, blk, stride=dil), :]

        def store(ref, start, val):
            if dil == 1:
                ref[pl.ds(start, blk), :] = val
            else:
                ref[pl.ds(start, blk, stride=dil), :] = val

        def block(base, prev, prev_cond):
            q = load(q_ref, base).astype(BF16)
            kc = load(k_ref, base).astype(BF16)
            vc = jnp.concatenate([load(v_ref, base).astype(BF16), ones], axis=1)
            s_c = jnp.where(cur_ok, lax.dot_general(q, kc, nt, preferred_element_type=F32), NEG_BIG)
            m_blk = jnp.max(s_c, axis=1, keepdims=True)
            if prev is not None:
                ok = prev_ok if prev_cond is None else jnp.logical_and(prev_ok, prev_cond)
                s_p = jnp.where(ok, lax.dot_general(q, prev[0], nt, preferred_element_type=F32), NEG_BIG)
                m_blk = jnp.maximum(m_blk, jnp.max(s_p, axis=1, keepdims=True))
            if first:
                m_new = m_blk
            else:
                m_old = load(m_scr, base)
                m_new = jnp.maximum(m_old, m_blk)
            pv = jnp.dot(jnp.exp(s_c - m_new).astype(BF16), vc, preferred_element_type=F32)
            if prev is not None:
                pv = pv + jnp.dot(jnp.exp(s_p - m_new).astype(BF16), prev[1], preferred_element_type=F32)
            acc_new = pv[:, :HEAD_DIM]
            l_new = pv[:, HEAD_DIM:]
            if not first:
                alpha = jnp.exp(m_old - m_new)
                acc_new = alpha * load(acc_scr, base) + acc_new
                l_new = alpha * load(l_scr, base) + l_new
            if last and dil == 1:
                o_ref[pl.ds(pl.multiple_of(base, blk), blk), :] = (acc_new / l_new).astype(o_ref.dtype)
            else:
                store(acc_scr, base, acc_new)
                store(l_scr, base, l_new)
                if not last:
                    store(m_scr, base, jnp.broadcast_to(m_new, (blk, blk)))
            return kc, vc

        if n_qb >= ATTN_UNROLL:
            assert n_qb % ATTN_UNROLL == 0

            def res_body(res, _):
                def body(it, carry):
                    prev = carry
                    for u in range(ATTN_UNROLL):
                        base = (it * ATTN_UNROLL + u) * (blk * dil) + res
                        prev = block(base, prev, (it > 0) if u == 0 else None)
                    return prev
                init = (jnp.zeros((blk, HEAD_DIM), BF16), jnp.zeros((blk, 2 * HEAD_DIM), BF16))
                lax.fori_loop(0, n_qb // ATTN_UNROLL, body, init)
                return 0
            if dil == 1:
                res_body(0, 0)
            else:
                lax.fori_loop(0, dil, res_body, 0)
        else:
            assert ATTN_UNROLL % n_qb == 0
            res_per_body = ATTN_UNROLL // n_qb
            assert dil % res_per_body == 0

            def body(it, _):
                for rr in range(res_per_body):
                    res = it * res_per_body + rr
                    prev = None
                    for u in range(n_qb):
                        prev = block(u * (blk * dil) + res, prev, None)
                return 0
            lax.fori_loop(0, dil // res_per_body, body, 0)

    for step_i, gi in enumerate(ATTN_ORDER):
        window, dil = DILATION_PATTERNS[gi]
        assert window // dil == blk
        pl.when(step == step_i)(functools.partial(run_group, step_i, dil))

    if DILATION_PATTERNS[ATTN_ORDER[-1]][1] != 1:
        @pl.when(step == n_g - 1)
        def _():
            o_ref[...] = (acc_scr[...] / l_scr[...]).astype(o_ref.dtype)


def dilated_attention(qk, v, n_batch, seq, n_heads, side=()):
    n_g = len(DILATION_PATTERNS)
    width = n_g * n_heads

    def group_of_step(s):
        gi = ATTN_ORDER[-1]
        for step_i in range(n_g - 2, -1, -1):
            gi = jnp.where(s == step_i, ATTN_ORDER[step_i], gi)
        return gi

    def col_spec(part):
        return pl.BlockSpec((seq, LANES), lambda b, h, s: (b, part * width + group_of_step(s) * n_heads + h))

    grid = (n_batch, n_heads, n_g)
    side_specs, side_shapes = _side_cast_specs(side, grid)
    return pl.pallas_call(
        _with_side_casts(functools.partial(_attn_kernel, seq=seq), 3, 1, len(side)),
        grid=grid,
        in_specs=[col_spec(0), col_spec(1), col_spec(0)] + side_specs,
        out_specs=[pl.BlockSpec((seq, LANES), lambda b, h, g: (b, h))] + side_specs,
        out_shape=[jax.ShapeDtypeStruct((n_batch * seq, n_heads * LANES), BF16)] + side_shapes,
        scratch_shapes=[pltpu.VMEM((seq, LANES), F32)] * 3,
        compiler_params=_params(3),
        name="dilated_attention",
    )(qk, qk, v, *[src for src, _ in side])


def _sigmoid(x):
    return 1.0 / (1.0 + jnp.exp(-x))


def _mix_kernel(y_ref, at_ref, wa_ref, wb_ref, wo_ref, gs_ref, ga_ref, o_ref):
    y = y_ref[...]
    attn = jnp.dot(at_ref[...], wo_ref[...], preferred_element_type=F32)
    attn_gated = _sigmoid(ga_ref[...].astype(F32)) * attn
    glu_b = jnp.dot(y, wb_ref[...], preferred_element_type=F32)
    ssm_gate = _sigmoid(gs_ref[...].astype(F32)) * _sigmoid(glu_b)
    glu_a = jnp.dot(y, wa_ref[...], preferred_element_type=F32)
    o_ref[...] = (ssm_gate * glu_a + attn_gated).astype(o_ref.dtype)


def gated_merge(y, attn, w_glu, w_ao, proj, col_gs, col_ga, tm, tn, side=()):
    m, ky = y.shape
    ka = attn.shape[1]
    d = w_ao.shape[1]
    nj = d // tn
    grid = (m // tm, nj)
    side_specs, side_shapes = _side_cast_specs(side, grid)
    return pl.pallas_call(
        _with_side_casts(_mix_kernel, 7, 1, len(side)),
        grid=grid,
        in_specs=[pl.BlockSpec((tm, ky), lambda i, j: (i, 0)),
                  pl.BlockSpec((tm, ka), lambda i, j: (i, 0)),
                  pl.BlockSpec((ky, tn), lambda i, j: (0, j)),
                  pl.BlockSpec((ky, tn), lambda i, j: (0, nj + j)),
                  pl.BlockSpec((ka, tn), lambda i, j: (0, j)),
                  pl.BlockSpec((tm, tn), lambda i, j: (i, col_gs // tn + j)),
                  pl.BlockSpec((tm, tn), lambda i, j: (i, col_ga // tn + j))] + side_specs,
        out_specs=[pl.BlockSpec((tm, tn), lambda i, j: (i, j))] + side_specs,
        out_shape=[jax.ShapeDtypeStruct((m, d), BF16)] + side_shapes,
        compiler_params=_params(2),
        name="gated_merge",
    )(y, attn, w_glu, w_glu, w_ao, proj, proj, *[src for src, _ in side])


def _ffn_up_kernel(h_ref, wa_ref, wg_ref, ca_ref, cg_ref, o_ref, ua_scr, ug_scr, halo_a, halo_g,
                   *, tm, blocks_per_seq):
    i = pl.program_id(0)
    j = pl.program_id(1)
    h = h_ref[...]
    seq_start = (i % blocks_per_seq) == 0

    def conv(w_ref, c_ref, u_scr, halo):
        u_scr[pl.ds(HALO_ROWS, tm), :] = jnp.dot(h, w_ref[...].astype(BF16), preferred_element_type=F32)
        u_scr[pl.ds(0, HALO_ROWS), :] = jnp.where(seq_start, 0.0, halo[j])
        halo[j] = u_scr[pl.ds(tm, HALO_ROWS), :]
        cw = c_ref[...]
        return (cw[2:3, :] * u_scr[pl.ds(HALO_ROWS, tm), :]
                + cw[1:2, :] * u_scr[pl.ds(HALO_ROWS - 1, tm), :]
                + cw[0:1, :] * u_scr[pl.ds(HALO_ROWS - 2, tm), :]
                + cw[3:4, :])

    a = conv(wa_ref, ca_ref, ua_scr, halo_a)
    silu_a = a * _sigmoid(a)
    gv = conv(wg_ref, cg_ref, ug_scr, halo_g)
    o_ref[...] = (silu_a * gv).astype(o_ref.dtype)


def ffn_up(h, w_up, conv_p, seq, tm, tn):
    m, k = h.shape
    f = w_up.shape[1] // 2
    nj = -(-f // tn)
    assert f % LANES == 0 and nj >= 2

    def col0(j, half=0):
        return pl.multiple_of(half * f + jnp.minimum(j * tn, f - tn), LANES)

    lane_tile = pl.Element(tn)
    u_scratch = pltpu.VMEM((tm + HALO_ROWS, tn), F32)
    halo_scratch = pltpu.VMEM((nj, HALO_ROWS, tn), F32)
    return pl.pallas_call(
        functools.partial(_ffn_up_kernel, tm=tm, blocks_per_seq=seq // tm),
        grid=(m // tm, nj),
        in_specs=[pl.BlockSpec((tm, k), lambda i, j: (i, 0)),
                  pl.BlockSpec((pl.Element(k), lane_tile), lambda i, j: (0, col0(j))),
                  pl.BlockSpec((pl.Element(k), lane_tile), lambda i, j: (0, col0(j, 1))),
                  pl.BlockSpec((pl.Element(HALO_ROWS), lane_tile), lambda i, j: (0, col0(j))),
                  pl.BlockSpec((pl.Element(HALO_ROWS), lane_tile), lambda i, j: (0, col0(j, 1)))],
        out_specs=pl.BlockSpec((tm, tn), lambda i, j: (i, j)),
        out_shape=jax.ShapeDtypeStruct((m, nj * tn), BF16),
        scratch_shapes=[u_scratch, u_scratch, halo_scratch, halo_scratch],
        compiler_params=_params(2),
        name="ffn_up",
    )(h, w_up, w_up, conv_p, conv_p)


def _ffn_down_kernel(a1_ref, a2_ref, w1_ref, w2_ref, r_ref, o_ref):
    o_ref[...] = (r_ref[...]
                  + jnp.dot(a1_ref[...], w1_ref[...], preferred_element_type=F32)
                  + jnp.dot(a2_ref[...], w2_ref[...], preferred_element_type=F32))


def ffn_down(act, up_tile, w, res, tm, tn):
    m, width = act.shape
    k, n = w.shape
    k1 = width - up_tile
    k2 = k - k1
    assert 0 < k2 <= up_tile and width % k2 == 0 and k1 % k2 == 0
    return pl.pallas_call(
        _ffn_down_kernel,
        grid=(m // tm, n // tn),
        in_specs=[pl.BlockSpec((tm, k1), lambda i, j: (i, 0)),
                  pl.BlockSpec((tm, k2), lambda i, j: (i, width // k2 - 1)),
                  pl.BlockSpec((k1, tn), lambda i, j: (0, j)),
                  pl.BlockSpec((k2, tn), lambda i, j: (k1 // k2, j)),
                  pl.BlockSpec((tm, tn), lambda i, j: (i, j))],
        out_specs=pl.BlockSpec((tm, tn), lambda i, j: (i, j)),
        out_shape=jax.ShapeDtypeStruct((m, n), F32),
        compiler_params=_params(2),
        name="ffn_down",
    )(act, act, w, w, res)


def kernel(x, g_mix, w_in, ssm_log_dt, ssm_a_re, ssm_a_im, ssm_b_re, ssm_b_im, ssm_c_re, ssm_c_im,
           ssm_d, w_glu, w_attn_out, w_out, g_ffn, w_up, conv_w, conv_b, w_down, g_final):
    b, l, d = x.shape
    depth = g_mix.shape[0]
    ssm_width = ssm_a_re.shape[1] * ssm_b_re.shape[-1]
    attn_width = w_attn_out.shape[1]
    n_heads = attn_width // HEAD_DIM
    qkv_width = len(DILATION_PATTERNS) * attn_width
    d_ff = w_down.shape[1]
    ff_tile = 512
    proj_tile = 1024
    merge_tile = 512
    row_tile = 1024
    n_row_tiles = b * l // row_tile
    ssm_tiles = ssm_width // proj_tile
    qkv_tiles = 3 * qkv_width // proj_tile
    gate_tiles = 2 * d // proj_tile

    xf = x.reshape(b * l, d).astype(F32)
    for i in range(depth):
        h = rmsnorm(xf, g_mix[i], BF16)
        w_in_b = w_in[i].astype(BF16)
        n_in_tiles = ssm_tiles + gate_tiles
        proj, w_up_b = matmul(h, w_in_b, BF16, row_tile, proj_tile, n_in_tiles,
                              lambda j: jnp.where(j < ssm_tiles, j, j + qkv_tiles), "in_proj",
                              side=[_cast_chunks(w_up[i], n_row_tiles * n_in_tiles)])
        (qk,) = qk_projection(h, w_in_b, ssm_tiles, qkv_width, l, row_tile, proj_tile)
        v_tile0 = ssm_tiles + 2 * qkv_width // proj_tile
        (v,) = matmul(h, w_in_b, F32, row_tile, proj_tile, qkv_width // proj_tile,
                      lambda j: v_tile0 + j, "v_proj")
        mats = s5_prepare(ssm_log_dt[i], ssm_a_re[i], ssm_a_im[i], ssm_b_re[i], ssm_b_im[i],
                          ssm_c_re[i], ssm_c_im[i], ssm_d[i])
        y = s5_mixer_gelu(proj, 0, b, l, mats)
        n_attn_steps = b * n_heads * len(DILATION_PATTERNS)
        attn, w_glu_b, w_ao_b, w_out_b = dilated_attention(
            qk, v, b, l, n_heads,
            side=[_cast_chunks(w_glu[i], n_attn_steps), _cast_chunks(w_attn_out[i], n_attn_steps),
                  _cast_chunks(w_out[i], n_attn_steps)])
        mixed, w_down_b = gated_merge(y, attn, w_glu_b, w_ao_b, proj, ssm_width, ssm_width + d,
                                      row_tile, merge_tile,
                                      side=[_cast_chunks(w_down[i], n_row_tiles * (d // merge_tile))])
        xf, h = matmul_residual_norm(mixed, w_out_b, xf, g_ffn[i], 512, 512, "out_proj")

        zrow = jnp.zeros((HALO_ROWS - CONV_WIDTH - 1, 2 * d_ff), F32)
        conv_p = jnp.concatenate([conv_w[i].astype(F32), conv_b[i][None].astype(F32), zrow], axis=0)
        act = ffn_up(h, w_up_b, conv_p, l, row_tile, ff_tile)
        xf = ffn_down(act, ff_tile, w_down_b, xf, 512, 512)
    out = rmsnorm(xf, g_final, x.dtype)
    return out.reshape(b, l, d)
```
